```python
import math
import jax, jax.numpy as jnp
from jax import lax
import numpy as np

D_MODEL = 2048
BATCH = 4
SEQ = 4096
DEPTH = 1
DEC_BATCH = 32
DEC_SEQ = 64
PAST_LEN = 2048

CHUNK = 64
D_MIX = D_MODEL
D_RWKV = D_MIX // 2
D_S5 = D_MIX - D_RWKV
HEAD_SIZE = 64
N_HEADS = D_RWKV // HEAD_SIZE
DECAY_LORA = 64
AAA_LORA = 64
GATE_LORA = 160
N_SHIFT = 3 * D_RWKV + DECAY_LORA + AAA_LORA + GATE_LORA
N_IN = N_SHIFT + D_S5
S5_GROUP = 16
S5_GROUPS = D_S5 // S5_GROUP
S5_STATE = 64
D_FF = -(-8 * D_MODEL // (3 * 256)) * 256
ALPHA = (2 * DEPTH) ** 0.25
BETA = (8 * DEPTH) ** -0.25
LN_EPS = 1e-5
GN_EPS = 64e-5

kernel_name = 'rwkv7_s5_hymba_streaming_step'


def layer_norm(x, g, b):
    xf = x.astype(jnp.float32)
    mu = xf.mean(-1, keepdims=True)
    var = jnp.square(xf - mu).mean(-1, keepdims=True)
    y = (xf - mu) * lax.rsqrt(var + LN_EPS) * g.astype(jnp.float32) + b.astype(jnp.float32)
    return y.astype(x.dtype)


def wkv7_scan(r, decay, k, v, kk, a, s0):
    def step(s, inp):
        r_t, w_t, k_t, v_t, kk_t, a_t = inp
        sa = jnp.einsum('bhij,bhj->bhi', s, kk_t)
        s = (s * w_t[:, :, None, :]
             - sa[..., None] * (kk_t * a_t)[:, :, None, :]
             + v_t[..., None] * k_t[:, :, None, :])
        return s, jnp.einsum('bhij,bhj->bhi', s, r_t)
    xs = tuple(jnp.moveaxis(t, 1, 0) for t in (r, decay, k, v, kk, a))
    s_last, ys = lax.scan(step, s0, xs)
    return jnp.moveaxis(ys, 0, 1), s_last


def _affine_combine(e1, e2):
    a1, b1 = e1
    a2, b2 = e2
    return a1 * a2, a2 * b1 + b2


def s5_scan(u, lam_bar, b_bar, c_mat, h0):
    bsz, seq = u.shape[:2]
    c = min(CHUNK, seq)
    n_chunks = seq // c
    u_ch = jnp.moveaxis(u.reshape(bsz, n_chunks, c, S5_GROUPS, S5_GROUP), 1, 0)

    def chunk_step(h, u_c):
        bu = jnp.einsum('bcgk,gpk->bcgp', u_c.astype(jnp.complex64), b_bar)
        bu = bu.at[:, 0].add(lam_bar * h)
        dec = jnp.broadcast_to(lam_bar, bu.shape)
        _, hs = lax.associative_scan(_affine_combine, (dec, bu), axis=1)
        y = jnp.einsum('bcgp,gkp->bcgk', hs, c_mat).real
        return hs[:, -1], y

    h_last, ys = lax.scan(chunk_step, h0, u_ch)
    return jnp.moveaxis(ys, 0, 1).reshape(u.shape), h_last


def mixer(xn, shift0, s0, h0, w_in, mu_shift, w0, w_lora_up, a0, a_lora_up, g_lora_up,
          k_k, k_a, r_k, gn_g, gn_b, a_re, a_im, log_dt, b_re, b_im, c_re, c_im,
          s5_d, glu_w, glu_b, w_out):
    f32 = lambda t: t.astype(jnp.float32)
    bsz, seq, _ = xn.shape
    proj = jnp.einsum('bld,de->ble', xn, w_in)
    p_rw = proj[..., :N_SHIFT]
    u = proj[..., N_SHIFT:]
    prev = jnp.concatenate([shift0.astype(p_rw.dtype), p_rw[:, :-1]], axis=1)
    p_s = p_rw + (prev - p_rw) * mu_shift
    new_shift = p_rw[:, -1:]
    o1 = D_RWKV
    o2 = 2 * D_RWKV
    o3 = 3 * D_RWKV
    o4 = o3 + DECAY_LORA
    o5 = o4 + AAA_LORA
    r, k, v, xw, xa, xg = jnp.split(f32(p_s), [o1, o2, o3, o4, o5], axis=-1)
    w = f32(w0) + jnp.tanh(xw) @ f32(w_lora_up)
    w = -jax.nn.softplus(-w) - 0.5
    decay = jnp.exp(-jnp.exp(w))
    a = jax.nn.sigmoid(f32(a0) + xa @ f32(a_lora_up))
    g = jax.nn.sigmoid(xg) @ f32(g_lora_up)
    heads = lambda t: t.reshape(bsz, seq, N_HEADS, HEAD_SIZE)
    kk = heads(k * f32(k_k))
    kk = kk / jnp.maximum(jnp.sqrt(jnp.sum(kk * kk, -1, keepdims=True)), 1e-12)
    k = k * (1.0 + (a - 1.0) * f32(k_a))
    rh, kh, vh = heads(r), heads(k), heads(v)
    yh, s_last = wkv7_scan(rh, heads(decay), kh, vh, kk, heads(a), f32(s0))
    mu = yh.mean(-1, keepdims=True)
    var = jnp.square(yh - mu).mean(-1, keepdims=True)
    yn = ((yh - mu) * lax.rsqrt(var + GN_EPS)).reshape(bsz, seq, D_RWKV) * f32(gn_g) + f32(gn_b)
    bonus = (jnp.sum(rh * kh * f32(r_k), -1, keepdims=True) * vh).reshape(bsz, seq, D_RWKV)
    o_rwkv = (yn + bonus) * g
    dt = jnp.exp(f32(log_dt))[:, None]
    lam = lax.complex(f32(a_re), f32(a_im))
    lam_bar = jnp.exp(lam * dt)
    b_bar = ((lam_bar - 1.0) / lam)[..., None] * lax.complex(f32(b_re), f32(b_im))
    c_mat = lax.complex(f32(c_re), f32(c_im))
    ug = f32(u).reshape(bsz, seq, S5_GROUPS, S5_GROUP)
    ys, h_last = s5_scan(ug, lam_bar, b_bar, c_mat, h0)
    ys = ys + f32(s5_d) * ug
    z = jax.nn.gelu(ys.reshape(bsz, seq, D_S5))
    o_s5 = z * jax.nn.sigmoid(z @ f32(glu_w) + f32(glu_b))
    o = jnp.concatenate([o_rwkv, o_s5], axis=-1).astype(xn.dtype)
    return o @ w_out, new_shift, s_last, h_last


def swiglu(x, w1, w3, w2):
    return (jax.nn.silu(x @ w1) * (x @ w3)) @ w2


def _normal(key, shape, scale):
    return jax.random.normal(key, shape, jnp.float32) * scale


def setup_inputs(seed: int = 0) -> dict:
    key = jax.random.key(seed)
    ks = iter(jax.random.split(key, 48))
    nk = lambda: next(ks)
    f = jnp.float32
    L = DEPTH
    w0_base = jnp.linspace(-6.0, -1.0, D_RWKV, dtype=f)
    a_im_base = math.pi * jnp.arange(S5_STATE, dtype=f)
    return {
        'x_prompt': _normal(nk(), (BATCH, SEQ, D_MODEL), 1.0),
        'x_sample': _normal(nk(), (DEC_BATCH, DEC_SEQ, D_MODEL), 1.0),
        'cache_shift': _normal(nk(), (L, DEC_BATCH, 1, N_SHIFT), 1.0),
        'state_wkv': _normal(nk(), (L, DEC_BATCH, N_HEADS, HEAD_SIZE, HEAD_SIZE), 0.5),
        'state_s5_re': _normal(nk(), (L, DEC_BATCH, S5_GROUPS, S5_STATE), 0.5),
        'state_s5_im': _normal(nk(), (L, DEC_BATCH, S5_GROUPS, S5_STATE), 0.5),
        'ln_in_g': 1.0 + _normal(nk(), (D_MODEL,), 0.02),
        'ln_in_b': _normal(nk(), (D_MODEL,), 0.02),
        'w_in': _normal(nk(), (L, D_MODEL, N_IN), D_MODEL ** -0.5),
        'mu_shift': jax.random.uniform(nk(), (L, N_SHIFT), f, 0.0, 1.0),
        'w0': w0_base + _normal(nk(), (L, D_RWKV), 0.1),
        'w_lora_up': _normal(nk(), (L, DECAY_LORA, D_RWKV), 0.1 * DECAY_LORA ** -0.5),
        'a0': _normal(nk(), (L, D_RWKV), 0.1),
        'a_lora_up': _normal(nk(), (L, AAA_LORA, D_RWKV), 0.1 * AAA_LORA ** -0.5),
        'g_lora_up': _normal(nk(), (L, GATE_LORA, D_RWKV), GATE_LORA ** -0.5),
        'k_k': 0.85 + _normal(nk(), (L, D_RWKV), 0.02),
        'k_a': 1.0 + _normal(nk(), (L, D_RWKV), 0.02),
        'r_k': _normal(nk(), (L, N_HEADS, HEAD_SIZE), 0.1),
        'gn_g': 1.0 + _normal(nk(), (L, D_RWKV), 0.02),
        'gn_b': _normal(nk(), (L, D_RWKV), 0.02),
        's5_a_re': -0.5 + _normal(nk(), (L, S5_GROUPS, S5_STATE), 0.01),
        's5_a_im': a_im_base + _normal(nk(), (L, S5_GROUPS, S5_STATE), 0.01),
        's5_log_dt': jax.random.uniform(nk(), (L, S5_GROUPS), f, math.log(1e-3), math.log(1e-1)),
        's5_b_re': _normal(nk(), (L, S5_GROUPS, S5_STATE, S5_GROUP), (2 * S5_GROUP) ** -0.5),
        's5_b_im': _normal(nk(), (L, S5_GROUPS, S5_STATE, S5_GROUP), (2 * S5_GROUP) ** -0.5),
        's5_c_re': _normal(nk(), (L, S5_GROUPS, S5_GROUP, S5_STATE), (2 * S5_STATE) ** -0.5),
        's5_c_im': _normal(nk(), (L, S5_GROUPS, S5_GROUP, S5_STATE), (2 * S5_STATE) ** -0.5),
        's5_d': _normal(nk(), (L, S5_GROUPS, S5_GROUP), 1.0),
        'glu_w': _normal(nk(), (L, D_S5, D_S5), D_S5 ** -0.5),
        'glu_b': _normal(nk(), (L, D_S5), 0.01),
        'w_out': _normal(nk(), (L, D_MIX, D_MODEL), BETA * D_MIX ** -0.5),
        'ln1_g': 1.0 + _normal(nk(), (L, D_MODEL), 0.02),
        'ln1_b': _normal(nk(), (L, D_MODEL), 0.02),
        'ffn_w1': _normal(nk(), (L, D_MODEL, D_FF), D_MODEL ** -0.5),
        'ffn_w3': _normal(nk(), (L, D_MODEL, D_FF), D_MODEL ** -0.5),
        'ffn_w2': _normal(nk(), (L, D_FF, D_MODEL), BETA * D_FF ** -0.5),
        'ln2_g': 1.0 + _normal(nk(), (L, D_MODEL), 0.02),
        'ln2_b': _normal(nk(), (L, D_MODEL), 0.02),
    }


def reference(x_prompt, x_sample, cache_shift, state_wkv, state_s5_re, state_s5_im,
              ln_in_g, ln_in_b, w_in, mu_shift, w0, w_lora_up, a0, a_lora_up, g_lora_up,
              k_k, k_a, r_k, gn_g, gn_b, s5_a_re, s5_a_im, s5_log_dt, s5_b_re, s5_b_im,
              s5_c_re, s5_c_im, s5_d, glu_w, glu_b, w_out, ln1_g, ln1_b,
              ffn_w1, ffn_w3, ffn_w2, ln2_g, ln2_b):
    assert x_sample.shape[1] <= CHUNK

    def run(x, shift, wkv, h_re, h_im):
        x = layer_norm(x, ln_in_g, ln_in_b)
        outs = []
        for l in range(DEPTH):
            h0 = lax.complex(h_re[l].astype(jnp.float32), h_im[l].astype(jnp.float32))
            mix, n_shift, n_wkv, n_h = mixer(
                x, shift[l], wkv[l], h0, w_in[l], mu_shift[l], w0[l], w_lora_up[l], a0[l],
                a_lora_up[l], g_lora_up[l], k_k[l], k_a[l], r_k[l], gn_g[l], gn_b[l],
                s5_a_re[l], s5_a_im[l], s5_log_dt[l], s5_b_re[l], s5_b_im[l], s5_c_re[l],
                s5_c_im[l], s5_d[l], glu_w[l], glu_b[l], w_out[l])
            x = layer_norm(ALPHA * x + mix, ln1_g[l], ln1_b[l])
            x = layer_norm(ALPHA * x + swiglu(x, ffn_w1[l], ffn_w3[l], ffn_w2[l]), ln2_g[l], ln2_b[l])
            outs.append((n_shift, n_wkv, jnp.real(n_h), jnp.imag(n_h)))
        dt = x.dtype
        stack = lambda i: jnp.stack([o[i] for o in outs]).astype(dt)
        return x, stack(0), stack(1), stack(2), stack(3)

    bp = x_prompt.shape[0]
    dtp = x_prompt.dtype
    y_prompt, shift_p, wkv_p, s5re_p, s5im_p = run(
        x_prompt,
        jnp.zeros((DEPTH, bp, 1, N_SHIFT), dtp),
        jnp.zeros((DEPTH, bp, N_HEADS, HEAD_SIZE, HEAD_SIZE), dtp),
        jnp.zeros((DEPTH, bp, S5_GROUPS, S5_STATE), dtp),
        jnp.zeros((DEPTH, bp, S5_GROUPS, S5_STATE), dtp))
    y_sample, shift_s, wkv_s, s5re_s, s5im_s = run(
        x_sample, cache_shift, state_wkv, state_s5_re, state_s5_im)
    return (y_prompt, y_sample, shift_p, wkv_p, s5re_p, s5im_p, shift_s, wkv_s, s5re_s, s5im_s)
```

```python
import functools
import math

import jax
import jax.numpy as jnp
from jax import lax
from jax.experimental import pallas as pl
from jax.experimental.pallas import tpu as pltpu

F32 = jnp.float32
BF16 = jnp.bfloat16

D_MODEL = 2048
D_RWKV = 1024
D_S5 = 1024
HEAD = 64
N_HEADS = D_RWKV // HEAD
LORA_W = 64
LORA_A = 64
LORA_G = 160
N_LORA = LORA_W + LORA_A + LORA_G
N_LORA_PAD = 512
N_SHIFT = 3 * D_RWKV + N_LORA
N_SHIFT_PAD = 3 * D_RWKV + N_LORA_PAD
N_PROJ_PAD = 3 * D_RWKV + D_S5 + N_LORA_PAD
COL_U = 3
COL_LORA = 8
S5_GROUP = 16
S5_GROUPS = D_S5 // S5_GROUP
S5_STATE = 64
S5_WIDE = S5_GROUPS * S5_STATE
S5_BUNDLE = 8
CHUNK = 64
LN_EPS = 1e-5
GN_EPS = 64e-5
ALPHA = 2.0 ** 0.25
VMEM_LIMIT = 56 * 1024 * 1024


def _split(x):
    hi = x.astype(BF16)
    lo = (x - hi.astype(F32)).astype(BF16)
    return hi, lo


def _dot(a, b, dims=((1,), (0,))):
    return lax.dot_general(a, b, (dims, ((), ())), preferred_element_type=F32)


def _mm(a, b, dims=((1,), (0,)), passes=1):
    if passes == 1:
        return _dot(a.astype(BF16), b.astype(BF16), dims)
    ah, al = _split(a)
    if passes == 2:
        bh = b.astype(BF16)
        return _dot(ah, bh, dims) + _dot(al, bh, dims)
    bh, bl = _split(b)
    return _dot(ah, bh, dims) + (_dot(al, bh, dims) + _dot(ah, bl, dims))


def _layer_norm(x, g, b):
    mu = jnp.mean(x, axis=-1, keepdims=True)
    xc = x - mu
    var = jnp.mean(xc * xc, axis=-1, keepdims=True)
    return xc * lax.rsqrt(var + LN_EPS) * g + b


def _row_tile(t, cap):
    for c in (1024, 512, 256, 128, 64):
        if c <= cap and t % c == 0:
            return c
    raise ValueError(f"token count {t} is not a multiple of 64")


def _const_spec(arr):
    nd = arr.ndim
    return pl.BlockSpec(arr.shape, lambda *_: (0,) * nd)


def _ln_proj_kernel(x_ref, g_ref, b_ref, w_ref, o_ref, xn_ref):
    @pl.when(pl.program_id(1) == 0)
    def _():
        xn_ref[...] = _layer_norm(x_ref[...], g_ref[...], b_ref[...]).astype(BF16)

    o_ref[...] = _dot(xn_ref[...], w_ref[...])


def _ln_proj(x2d, g, b, w_bf16):
    t, d = x2d.shape
    n = w_bf16.shape[1]
    tm = _row_tile(t, 1024)
    tn = 512
    return pl.pallas_call(
        _ln_proj_kernel,
        grid=(t // tm, n // tn),
        in_specs=[
            pl.BlockSpec((tm, d), lambda i, j: (i, 0)),
            pl.BlockSpec((1, d), lambda i, j: (0, 0)),
            pl.BlockSpec((1, d), lambda i, j: (0, 0)),
            pl.BlockSpec((d, tn), lambda i, j: (0, j)),
        ],
        out_specs=pl.BlockSpec((tm, tn), lambda i, j: (i, j)),
        out_shape=jax.ShapeDtypeStruct((t, n), F32),
        scratch_shapes=[pltpu.VMEM((tm, d), BF16)],
        compiler_params=pltpu.CompilerParams(
            dimension_semantics=("parallel", "arbitrary"), vmem_limit_bytes=VMEM_LIMIT),
    )(x2d, g, b, w_bf16)


def _segsum(x, ones):
    outs = []
    for q in range(x.shape[1] // 256):
        hi, lo = _split(x[:, 256 * q:256 * (q + 1)])
        outs.append(_dot(hi, ones) + _dot(lo, ones))
    return jnp.concatenate(outs, axis=1)


def _rwkv_kernel(pr_ref, pk_ref, pv_ref, plo_ref, sh_ref, s0_ref, mu_ref, w0_ref, wup_ref,
                 a0_ref, aup_ref, gup_ref, kkw_ref, kaw_ref, rkw_ref, gng_ref, gnb_ref,
                 tri_ref, ones_ref,
                 o_ref, st_ref,
                 carry_ref, r_s, k_s, v_s, kk_s, b_s, ld_s, lc_s, g_s, y_s,
                 *, tb, scan_passes):
    @pl.when(pl.program_id(1) == 0)
    def _():
        carry_ref[...] = sh_ref[...]
        st_ref[...] = s0_ref[...]

    row = lax.broadcasted_iota(jnp.int32, (tb, 1), 0)

    def token_shift(x, c0):
        width = x.shape[1]
        prev = pltpu.roll(x, 1, 0)
        prev = jnp.where(row == 0, carry_ref[:, c0:c0 + width], prev)
        shifted = x + (prev - x) * mu_ref[:, c0:c0 + width]
        carry_ref[:, c0:c0 + width] = x[tb - 1:tb, :]
        return shifted

    ones = ones_ref[...]
    r = token_shift(pr_ref[...], 0)
    k = token_shift(pk_ref[...], D_RWKV)
    v = token_shift(pv_ref[...], 2 * D_RWKV)
    lo = token_shift(plo_ref[...], 3 * D_RWKV)

    lane = lax.broadcasted_iota(jnp.int32, (tb, 128), 1)
    lo_wa = lo[:, 0:128]
    act_wa = jnp.where(lane < LORA_W, jnp.tanh(lo_wa), lo_wa).astype(BF16)
    act_g = jax.nn.sigmoid(lo[:, 128:384]).astype(BF16)
    w = w0_ref[...] + _dot(act_wa, wup_ref[...])
    z = -w
    w = -(jnp.maximum(z, 0.0) + jnp.log1p(jnp.exp(-jnp.abs(z)))) - 0.5
    ld = -jnp.exp(w)
    a = jax.nn.sigmoid(a0_ref[...] + _dot(act_wa, aup_ref[...]))
    g_s[...] = _dot(act_g, gup_ref[...])

    kk = k * kkw_ref[...]
    kk = kk / jnp.maximum(jnp.sqrt(_segsum(kk * kk, ones)), 1e-12)
    k = k * (1.0 + (a - 1.0) * kaw_ref[...])

    tri = tri_ref[...]
    h1 = ld.astype(BF16)
    r1 = ld - h1.astype(F32)
    h2 = r1.astype(BF16)
    h3 = (r1 - h2.astype(F32)).astype(BF16)
    lc_s[...] = _dot(tri, h1) + (_dot(tri, h2) + _dot(tri, h3))
    ld_s[...] = ld
    r_s[...] = r
    k_s[...] = k
    v_s[...] = v
    kk_s[...] = kk
    b_s[...] = kk * a

    rr = lax.broadcasted_iota(jnp.int32, (2 * CHUNK, 2 * CHUNK), 0)
    cc = lax.broadcasted_iota(jnp.int32, (2 * CHUNK, 2 * CHUNK), 1)
    tt = rr % CHUNK
    ss = cc % CHUNK
    causal = ss < tt + jnp.where(rr < CHUNK, 1, 0)
    lane_z = lax.broadcasted_iota(jnp.int32, (CHUNK, 4 * HEAD), 1)
    mm = functools.partial(_mm, passes=scan_passes)

    def chunk_body(c, _):
        c0 = pl.multiple_of(c * CHUNK, CHUNK)
        sl = pl.ds(c0, CHUNK)
        lc = lc_s[sl, :]
        ldc = ld_s[sl, :]
        e_inc = jnp.exp(lc)
        e_exc = jnp.exp(lc - ldc)
        e_inv = jnp.exp(-lc)
        lc_end = lc[CHUNK - 1:CHUNK, :]
        e_end = jnp.exp(lc_end)
        e_hat = jnp.exp(lc_end - lc)
        kc = k_s[sl, :]
        bc = b_s[sl, :]
        vc = v_s[sl, :]
        rt = r_s[sl, :] * e_inc
        kap = kk_s[sl, :] * e_exc
        kt = kc * e_inv
        bt = bc * e_inv
        khat = kc * e_hat
        bhat = bc * e_hat

        for h in range(N_HEADS):
            hs = slice(HEAD * h, HEAD * (h + 1))
            lhs = jnp.concatenate([rt[:, hs], kap[:, hs]], axis=0)
            rhs = jnp.concatenate([kt[:, hs], bt[:, hs]], axis=0)
            gm = jnp.where(causal, mm(lhs, rhs, ((1,), (1,))), 0.0)
            v_h = vc[:, hs]
            akk_v = mm(gm[CHUNK:, 0:HEAD], v_h)
            n_mat = -gm[CHUNK:, HEAD:2 * HEAD]
            zed = jnp.concatenate(
                [n_mat, akk_v, kap[:, hs], jnp.zeros((CHUNK, HEAD), F32)], axis=1)
            for _ in range(5):
                res = mm(zed[:, 0:HEAD], zed)
                zed = jnp.where(lane_z < HEAD, res, zed + res)
            res = mm(zed[:, 0:HEAD], zed)
            xsol = zed + res
            u_loc = xsol[:, HEAD:2 * HEAD]
            w_mat = xsol[:, 2 * HEAD:3 * HEAD]
            s_h = st_ref[h]
            d = mm(jnp.concatenate([rt[:, hs], w_mat], axis=0), s_h, ((1,), (1,)))
            u = u_loc + d[CHUNK:, :]
            vu = jnp.concatenate([v_h, -u], axis=0)
            y_s[sl, hs] = d[:CHUNK, :] + mm(gm[:CHUNK, :], vu)
            kb_hat = jnp.concatenate([khat[:, hs], bhat[:, hs]], axis=0)
            st_ref[h] = s_h * e_end[:, hs] + mm(vu, kb_hat, ((0,), (0,)))
        return 0

    lax.fori_loop(0, tb // CHUNK, chunk_body, 0)

    y = y_s[...]
    mu = _segsum(y, ones) * (1.0 / HEAD)
    yc = y - mu
    var = _segsum(yc * yc, ones) * (1.0 / HEAD)
    yn = yc * lax.rsqrt(var + GN_EPS) * gng_ref[...] + gnb_ref[...]
    bonus = _segsum(r_s[...] * k_s[...] * rkw_ref[...], ones) * v_s[...]
    o_ref[...] = ((yn + bonus) * g_s[...]).astype(o_ref.dtype)


def _rwkv(proj, shift0p, wkv0, p, bsz, seq, scan_passes):
    tb = 256 if seq % 256 == 0 else CHUNK
    nl = seq // tb
    t = bsz * seq

    def col(cb):
        return pl.BlockSpec((tb, D_RWKV), lambda b, l, cb=cb: (b * nl + l, cb))

    consts = [p["mu_p"], p["w0"], p["wup"], p["a0"], p["aup"], p["gup"], p["k_k"], p["k_a"],
              p["r_k"], p["gn_g"], p["gn_b"], p["tri"][:tb, :tb], p["ones"]]
    scratch = [pltpu.VMEM((1, N_SHIFT_PAD), F32)] + [pltpu.VMEM((tb, D_RWKV), F32)] * 9
    return pl.pallas_call(
        functools.partial(_rwkv_kernel, tb=tb, scan_passes=scan_passes),
        grid=(bsz, nl),
        in_specs=[col(0), col(1), col(2),
                  pl.BlockSpec((tb, N_LORA_PAD), lambda b, l: (b * nl + l, COL_LORA)),
                  pl.BlockSpec((None, 1, N_SHIFT_PAD), lambda b, l: (b, 0, 0)),
                  pl.BlockSpec((None, N_HEADS, HEAD, HEAD), lambda b, l: (b, 0, 0, 0)),
                  ] + [_const_spec(c) for c in consts],
        out_specs=[pl.BlockSpec((tb, D_RWKV), lambda b, l: (b * nl + l, 0)),
                   pl.BlockSpec((None, N_HEADS, HEAD, HEAD), lambda b, l: (b, 0, 0, 0))],
        out_shape=[jax.ShapeDtypeStruct((t, D_RWKV), BF16),
                   jax.ShapeDtypeStruct((bsz, N_HEADS, HEAD, HEAD), F32)],
        scratch_shapes=scratch,
        compiler_params=pltpu.CompilerParams(
            dimension_semantics=("parallel", "arbitrary"), vmem_limit_bytes=VMEM_LIMIT),
    )(proj, proj, proj, proj, shift0p, wkv0, *consts)


def _s5_kernel(u_ref, hre0_ref, him0_ref, lre_ref, lim_ref, wbre_ref, wbim_ref, wcre_ref,
               wcim_ref, d_ref, gluw_ref, glub_ref,
               o_ref, hre_ref, him_ref,
               xre_s, xim_s, *, tb, in_passes, out_passes):
    @pl.when(pl.program_id(1) == 0)
    def _():
        hre_ref[...] = hre0_ref[...]
        him_ref[...] = him0_ref[...]

    u = u_ref[...]
    kin = S5_BUNDLE * S5_GROUP
    kst = S5_BUNDLE * S5_STATE
    for m in range(S5_GROUPS // S5_BUNDLE):
        um = u[:, kin * m:kin * (m + 1)]
        xre_s[:, kst * m:kst * (m + 1)] = _mm(um, wbre_ref[m], passes=in_passes)
        xim_s[:, kst * m:kst * (m + 1)] = _mm(um, wbim_ref[m], passes=in_passes)

    lre = lre_ref[...]
    lim = lim_ref[...]

    def step(t, carry):
        hre, him = carry
        row = pl.ds(t, 1)
        nre = lre * hre - lim * him + xre_s[row, :]
        nim = lre * him + lim * hre + xim_s[row, :]
        xre_s[row, :] = nre
        xim_s[row, :] = nim
        return nre, nim

    hre, him = lax.fori_loop(0, tb, step, (hre_ref[...], him_ref[...]), unroll=8)
    hre_ref[...] = hre
    him_ref[...] = him

    ys = []
    for m in range(S5_GROUPS // S5_BUNDLE):
        ys.append(_mm(xre_s[:, kst * m:kst * (m + 1)], wcre_ref[m], passes=out_passes)
                  + _mm(xim_s[:, kst * m:kst * (m + 1)], wcim_ref[m], passes=out_passes))
    y = jnp.concatenate(ys, axis=1) + d_ref[...] * u
    zg = 0.5 * y * (1.0 + jnp.tanh(math.sqrt(2.0 / math.pi) * (y + 0.044715 * (y * y * y))))
    gate = jax.nn.sigmoid(_dot(zg.astype(BF16), gluw_ref[...]) + glub_ref[...])
    o_ref[...] = (zg * gate).astype(o_ref.dtype)


def _s5(proj, hre0, him0, p, bsz, seq, in_passes, out_passes):
    tb = 256 if seq % 256 == 0 else CHUNK
    nl = seq // tb
    t = bsz * seq
    consts = [p["lam_re"], p["lam_im"], p["wb_re"], p["wb_im"], p["wc_re"], p["wc_im"],
              p["s5_d"], p["glu_w"], p["glu_b"]]
    state_spec = pl.BlockSpec((None, 1, S5_WIDE), lambda b, l: (b, 0, 0))
    return pl.pallas_call(
        functools.partial(_s5_kernel, tb=tb, in_passes=in_passes, out_passes=out_passes),
        grid=(bsz, nl),
        in_specs=[pl.BlockSpec((tb, D_S5), lambda b, l: (b * nl + l, COL_U)),
                  state_spec, state_spec] + [_const_spec(c) for c in consts],
        out_specs=[pl.BlockSpec((tb, D_S5), lambda b, l: (b * nl + l, 0)), state_spec, state_spec],
        out_shape=[jax.ShapeDtypeStruct((t, D_S5), BF16),
                   jax.ShapeDtypeStruct((bsz, 1, S5_WIDE), F32),
                   jax.ShapeDtypeStruct((bsz, 1, S5_WIDE), F32)],
        scratch_shapes=[pltpu.VMEM((tb, S5_WIDE), F32)] * 2,
        compiler_params=pltpu.CompilerParams(
            dimension_semantics=("parallel", "arbitrary"), vmem_limit_bytes=VMEM_LIMIT),
    )(proj, hre0, him0, *consts)


def _out_ln_kernel(x_ref, orw_ref, os5_ref, ging_ref, binb_ref, wa_ref, wb_ref, g_ref, b_ref, o_ref):
    xn = _layer_norm(x_ref[...], ging_ref[...], binb_ref[...])
    mix = _dot(orw_ref[...], wa_ref[...]) + _dot(os5_ref[...], wb_ref[...])
    o_ref[...] = _layer_norm(ALPHA * xn + mix, g_ref[...], b_ref[...])


def _out_ln(x2d, o_rw, o_s5, p):
    t, d = x2d.shape
    tm = _row_tile(t, 512)
    consts = [p["ln_in_g"], p["ln_in_b"], p["w_out_a"], p["w_out_b"], p["ln1_g"], p["ln1_b"]]
    return pl.pallas_call(
        _out_ln_kernel,
        grid=(t // tm,),
        in_specs=[pl.BlockSpec((tm, d), lambda i: (i, 0)),
                  pl.BlockSpec((tm, D_RWKV), lambda i: (i, 0)),
                  pl.BlockSpec((tm, D_S5), lambda i: (i, 0))] + [_const_spec(c) for c in consts],
        out_specs=pl.BlockSpec((tm, d), lambda i: (i, 0)),
        out_shape=jax.ShapeDtypeStruct((t, d), F32),
        compiler_params=pltpu.CompilerParams(
            dimension_semantics=("parallel",), vmem_limit_bytes=VMEM_LIMIT),
    )(x2d, o_rw, o_s5, *consts)


def _ffn_ln_kernel(x_ref, w1_ref, w3_ref, w2_ref, g_ref, b_ref, o_ref, xb_ref, *, nf):
    f = pl.program_id(1)

    @pl.when(f == 0)
    def _():
        xb_ref[...] = x_ref[...].astype(BF16)
        o_ref[...] = jnp.zeros_like(o_ref)

    xb = xb_ref[...]
    h1 = _dot(xb, w1_ref[...])
    h3 = _dot(xb, w3_ref[...])
    act = (h1 * jax.nn.sigmoid(h1) * h3).astype(BF16)
    o_ref[...] += _dot(act, w2_ref[...])

    @pl.when(f == nf - 1)
    def _():
        o_ref[...] = _layer_norm(ALPHA * x_ref[...] + o_ref[...], g_ref[...], b_ref[...])


def _ffn_ln(x1, p):
    t, d = x1.shape
    dff = p["ffn_w1"].shape[1]
    tm = _row_tile(t, 512)
    tf = 512
    nf = dff // tf
    return pl.pallas_call(
        functools.partial(_ffn_ln_kernel, nf=nf),
        grid=(t // tm, nf),
        in_specs=[pl.BlockSpec((tm, d), lambda i, f: (i, 0)),
                  pl.BlockSpec((d, tf), lambda i, f: (0, f)),
                  pl.BlockSpec((d, tf), lambda i, f: (0, f)),
                  pl.BlockSpec((tf, d), lambda i, f: (f, 0)),
                  pl.BlockSpec((1, d), lambda i, f: (0, 0)),
                  pl.BlockSpec((1, d), lambda i, f: (0, 0))],
        out_specs=pl.BlockSpec((tm, d), lambda i, f: (i, 0)),
        out_shape=jax.ShapeDtypeStruct((t, d), F32),
        scratch_shapes=[pltpu.VMEM((tm, d), BF16)],
        compiler_params=pltpu.CompilerParams(
            dimension_semantics=("parallel", "arbitrary"), vmem_limit_bytes=VMEM_LIMIT),
    )(x1, p["ffn_w1"], p["ffn_w3"], p["ffn_w2"], p["ln2_g"], p["ln2_b"])


def _pad_cols(x, n):
    return jnp.pad(x, [(0, 0)] * (x.ndim - 1) + [(0, n - x.shape[-1])])


def _shift_layout(x):
    return jnp.concatenate([x[..., :3 * D_RWKV], _pad_cols(x[..., 3 * D_RWKV:], N_LORA_PAD)], axis=-1)


def _block_diag(blocks):
    nb, g, a, b = blocks.shape
    eye = jnp.eye(g, dtype=blocks.dtype)
    return jnp.einsum("mgab,gh->mgahb", blocks, eye).reshape(nb, g * a, g * b)


def _prepare(ln_in_g, ln_in_b, w_in, mu_shift, w0, w_lora_up, a0, a_lora_up, g_lora_up, k_k, k_a,
             r_k, gn_g, gn_b, s5_a_re, s5_a_im, s5_log_dt, s5_b_re, s5_b_im, s5_c_re, s5_c_im,
             s5_d, glu_w, glu_b, w_out, ln1_g, ln1_b, ffn_w1, ffn_w3, ffn_w2, ln2_g, ln2_b):
    row = lambda x: x.reshape(1, -1).astype(F32)
    w = w_in[0]
    o3 = 3 * D_RWKV
    w_in_p = jnp.concatenate(
        [w[:, :o3], w[:, N_SHIFT:], _pad_cols(w[:, o3:N_SHIFT], N_LORA_PAD)], axis=1).astype(BF16)
    wup = jnp.zeros((128, D_RWKV), F32).at[:LORA_W].set(w_lora_up[0]).astype(BF16)
    aup = jnp.zeros((128, D_RWKV), F32).at[LORA_W:LORA_W + LORA_A].set(a_lora_up[0]).astype(BF16)
    gup = jnp.zeros((256, D_RWKV), F32).at[:LORA_G].set(g_lora_up[0]).astype(BF16)
    idx = jnp.arange(256)
    tri = ((idx[:, None] >= idx[None, :]) & (idx[:, None] // CHUNK == idx[None, :] // CHUNK)).astype(BF16)
    ones = (idx[:, None] // HEAD == idx[None, :] // HEAD).astype(BF16)

    dt = jnp.exp(s5_log_dt[0].astype(F32))[:, None]
    a_re, a_im = s5_a_re[0].astype(F32), s5_a_im[0].astype(F32)
    mag = jnp.exp(a_re * dt)
    lam_re, lam_im = mag * jnp.cos(a_im * dt), mag * jnp.sin(a_im * dt)
    den = a_re * a_re + a_im * a_im
    q_re = ((lam_re - 1.0) * a_re + lam_im * a_im) / den
    q_im = (lam_im * a_re - (lam_re - 1.0) * a_im) / den
    b_re, b_im = s5_b_re[0].astype(F32), s5_b_im[0].astype(F32)
    bb_re = q_re[..., None] * b_re - q_im[..., None] * b_im
    bb_im = q_re[..., None] * b_im + q_im[..., None] * b_re
    nb = S5_GROUPS // S5_BUNDLE
    to_in = lambda x: _block_diag(jnp.swapaxes(x, 1, 2).reshape(nb, S5_BUNDLE, S5_GROUP, S5_STATE))
    to_out = lambda x: _block_diag(jnp.swapaxes(x, 1, 2).reshape(nb, S5_BUNDLE, S5_STATE, S5_GROUP))
    return {
        "ln_in_g": row(ln_in_g), "ln_in_b": row(ln_in_b), "w_in_p": w_in_p,
        "mu_p": _shift_layout(row(mu_shift[0])), "w0": row(w0[0]), "wup": wup, "a0": row(a0[0]),
        "aup": aup, "gup": gup, "k_k": row(k_k[0]), "k_a": row(k_a[0]), "r_k": row(r_k[0]),
        "gn_g": row(gn_g[0]), "gn_b": row(gn_b[0]), "tri": tri, "ones": ones,
        "lam_re": lam_re.reshape(1, -1), "lam_im": lam_im.reshape(1, -1),
        "wb_re": to_in(bb_re), "wb_im": to_in(bb_im),
        "wc_re": to_out(s5_c_re[0].astype(F32)), "wc_im": to_out(-s5_c_im[0].astype(F32)),
        "s5_d": row(s5_d[0]), "glu_w": glu_w[0].astype(BF16), "glu_b": row(glu_b[0]),
        "w_out_a": w_out[0, :D_RWKV].astype(BF16), "w_out_b": w_out[0, D_RWKV:].astype(BF16),
        "ln1_g": row(ln1_g[0]), "ln1_b": row(ln1_b[0]),
        "ffn_w1": ffn_w1[0].astype(BF16), "ffn_w3": ffn_w3[0].astype(BF16),
        "ffn_w2": ffn_w2[0].astype(BF16), "ln2_g": row(ln2_g[0]), "ln2_b": row(ln2_b[0]),
    }


SCAN_PASSES = 3
S5_IN_PASSES = 3
S5_OUT_PASSES = 3


def _run(x, shift, wkv, h_re, h_im, p):
    bsz, seq, d = x.shape
    x2d = x.reshape(bsz * seq, d)
    proj = _ln_proj(x2d, p["ln_in_g"], p["ln_in_b"], p["w_in_p"])
    o_rw, n_wkv = _rwkv(proj, _shift_layout(shift[0]), wkv[0], p, bsz, seq, SCAN_PASSES)
    o_s5, n_hre, n_him = _s5(proj, h_re[0].reshape(bsz, 1, S5_WIDE), h_im[0].reshape(bsz, 1, S5_WIDE),
                             p, bsz, seq, S5_IN_PASSES, S5_OUT_PASSES)
    x1 = _out_ln(x2d, o_rw, o_s5, p)
    y = _ffn_ln(x1, p).reshape(bsz, seq, d)
    last = proj.reshape(bsz, seq, N_PROJ_PAD)[:, seq - 1:, :]
    lo0 = COL_LORA * N_LORA_PAD
    n_shift = jnp.concatenate([last[..., :3 * D_RWKV], last[..., lo0:lo0 + N_LORA]], axis=-1)
    return (y, n_shift[None], n_wkv[None],
            n_hre.reshape(1, bsz, S5_GROUPS, S5_STATE), n_him.reshape(1, bsz, S5_GROUPS, S5_STATE))


def kernel(x_prompt, x_sample, cache_shift, state_wkv, state_s5_re, state_s5_im, ln_in_g, ln_in_b, w_in, mu_shift, w0, w_lora_up, a0, a_lora_up, g_lora_up, k_k, k_a, r_k, gn_g, gn_b, s5_a_re, s5_a_im, s5_log_dt, s5_b_re, s5_b_im, s5_c_re, s5_c_im, s5_d, glu_w, glu_b, w_out, ln1_g, ln1_b, ffn_w1, ffn_w3, ffn_w2, ln2_g, ln2_b):
    p = _prepare(ln_in_g, ln_in_b, w_in, mu_shift, w0, w_lora_up, a0, a_lora_up, g_lora_up, k_k,
                 k_a, r_k, gn_g, gn_b, s5_a_re, s5_a_im, s5_log_dt, s5_b_re, s5_b_im, s5_c_re,
                 s5_c_im, s5_d, glu_w, glu_b, w_out, ln1_g, ln1_b, ffn_w1, ffn_w3, ffn_w2, ln2_g,
                 ln2_b)
    bp = x_prompt.shape[0]
    dt = x_prompt.dtype
    zeros = lambda *s: jnp.zeros((1, bp) + s, dt)
    yp, sh_p, wkv_p, re_p, im_p = _run(
        x_prompt, zeros(1, N_SHIFT), zeros(N_HEADS, HEAD, HEAD), zeros(S5_GROUPS, S5_STATE),
        zeros(S5_GROUPS, S5_STATE), p)
    ys, sh_s, wkv_s, re_s, im_s = _run(x_sample, cache_shift, state_wkv, state_s5_re, state_s5_im, p)
    return (yp, ys, sh_p, wkv_p, re_p, im_p, sh_s, wkv_s, re_s, im_s)
```

```python
import functools
import math

import jax
import jax.numpy as jnp
from jax import lax
from jax.experimental import pallas as pl
from jax.experimental.pallas import tpu as pltpu

F32 = jnp.float32
BF16 = jnp.bfloat16

D_MODEL = 2048
D_RWKV = 1024
D_S5 = 1024
HEAD = 64
N_HEADS = D_RWKV // HEAD
LORA_W = 64
LORA_A = 64
LORA_G = 160
N_LORA = LORA_W + LORA_A + LORA_G
N_LORA_PAD = 512
N_SHIFT = 3 * D_RWKV + N_LORA
N_SHIFT_PAD = 3 * D_RWKV + N_LORA_PAD
N_PROJ_PAD = 3 * D_RWKV + D_S5 + N_LORA_PAD
COL_U = 3
COL_LORA = 8
S5_GROUP = 16
S5_GROUPS = D_S5 // S5_GROUP
S5_STATE = 64
S5_WIDE = S5_GROUPS * S5_STATE
S5_BUNDLE = 8
CHUNK = 64
LN_EPS = 1e-5
GN_EPS = 64e-5
ALPHA = 2.0 ** 0.25
VMEM_LIMIT = 56 * 1024 * 1024


def _split(x):
    hi = x.astype(BF16)
    lo = (x - hi.astype(F32)).astype(BF16)
    return hi, lo


def _dot(a, b, dims=((1,), (0,))):
    return lax.dot_general(a, b, (dims, ((), ())), preferred_element_type=F32)


def _mm(a, b, dims=((1,), (0,)), passes=1):
    if passes == 1:
        return _dot(a.astype(BF16), b.astype(BF16), dims)
    ah, al = _split(a)
    if passes == 2:
        bh = b.astype(BF16)
        return _dot(ah, bh, dims) + _dot(al, bh, dims)
    bh, bl = _split(b)
    return _dot(ah, bh, dims) + (_dot(al, bh, dims) + _dot(ah, bl, dims))


def _layer_norm(x, g, b):
    mu = jnp.mean(x, axis=-1, keepdims=True)
    xc = x - mu
    var = jnp.mean(xc * xc, axis=-1, keepdims=True)
    return xc * lax.rsqrt(var + LN_EPS) * g + b


def _row_tile(t, cap):
    for c in (1024, 512, 256, 128, 64):
        if c <= cap and t % c == 0:
            return c
    raise ValueError(f"token count {t} is not a multiple of 64")


def _const_spec(arr):
    nd = arr.ndim
    return pl.BlockSpec(arr.shape, lambda *_: (0,) * nd)


def _ln_proj_kernel(x_ref, g_ref, b_ref, w_ref, o_ref, xn_ref):
    @pl.when(pl.program_id(1) == 0)
    def _():
        xn_ref[...] = _layer_norm(x_ref[...], g_ref[...], b_ref[...]).astype(BF16)

    o_ref[...] = _dot(xn_ref[...], w_ref[...])


def _ln_proj(x2d, g, b, w_bf16):
    t, d = x2d.shape
    n = w_bf16.shape[1]
    tm = _row_tile(t, 1024)
    tn = 512
    return pl.pallas_call(
        _ln_proj_kernel,
        grid=(t // tm, n // tn),
        in_specs=[
            pl.BlockSpec((tm, d), lambda i, j: (i, 0)),
            pl.BlockSpec((1, d), lambda i, j: (0, 0)),
            pl.BlockSpec((1, d), lambda i, j: (0, 0)),
            pl.BlockSpec((d, tn), lambda i, j: (0, j)),
        ],
        out_specs=pl.BlockSpec((tm, tn), lambda i, j: (i, j)),
        out_shape=jax.ShapeDtypeStruct((t, n), F32),
        scratch_shapes=[pltpu.VMEM((tm, d), BF16)],
        compiler_params=pltpu.CompilerParams(
            dimension_semantics=("parallel", "arbitrary"), vmem_limit_bytes=VMEM_LIMIT),
    )(x2d, g, b, w_bf16)


def _segsum(x, ones):
    outs = []
    for q in range(x.shape[1] // 256):
        hi, lo = _split(x[:, 256 * q:256 * (q + 1)])
        outs.append(_dot(hi, ones) + _dot(lo, ones))
    return jnp.concatenate(outs, axis=1)


def _rwkv_kernel(pr_ref, pk_ref, pv_ref, plo_ref, sh_ref, s0_ref, mu_ref, w0_ref, wup_ref,
                 a0_ref, aup_ref, gup_ref, kkw_ref, kaw_ref, rkw_ref, gng_ref, gnb_ref,
                 tri_ref, ones_ref,
                 o_ref, st_ref,
                 carry_ref, r_s, k_s, v_s, kk_s, b_s, ld_s, lc_s, g_s, y_s,
                 *, tb, scan_passes):
    @pl.when(pl.program_id(1) == 0)
    def _():
        carry_ref[...] = sh_ref[...]
        st_ref[...] = s0_ref[...]

    row = lax.broadcasted_iota(jnp.int32, (tb, 1), 0)

    def token_shift(x, c0):
        width = x.shape[1]
        prev = pltpu.roll(x, 1, 0)
        prev = jnp.where(row == 0, carry_ref[:, c0:c0 + width], prev)
        shifted = x + (prev - x) * mu_ref[:, c0:c0 + width]
        carry_ref[:, c0:c0 + width] = x[tb - 1:tb, :]
        return shifted

    ones = ones_ref[...]
    r = token_shift(pr_ref[...], 0)
    k = token_shift(pk_ref[...], D_RWKV)
    v = token_shift(pv_ref[...], 2 * D_RWKV)
    lo = token_shift(plo_ref[...], 3 * D_RWKV)

    lane = lax.broadcasted_iota(jnp.int32, (tb, 128), 1)
    lo_wa = lo[:, 0:128]
    act_wa = jnp.where(lane < LORA_W, jnp.tanh(lo_wa), lo_wa).astype(BF16)
    act_g = jax.nn.sigmoid(lo[:, 128:384]).astype(BF16)
    w = w0_ref[...] + _dot(act_wa, wup_ref[...])
    z = -w
    w = -(jnp.maximum(z, 0.0) + jnp.log1p(jnp.exp(-jnp.abs(z)))) - 0.5
    ld = -jnp.exp(w)
    a = jax.nn.sigmoid(a0_ref[...] + _dot(act_wa, aup_ref[...]))
    g_s[...] = _dot(act_g, gup_ref[...])

    kk = k * kkw_ref[...]
    kk = kk / jnp.maximum(jnp.sqrt(_segsum(kk * kk, ones)), 1e-12)
    k = k * (1.0 + (a - 1.0) * kaw_ref[...])

    tri = tri_ref[...]
    h1 = ld.astype(BF16)
    r1 = ld - h1.astype(F32)
    h2 = r1.astype(BF16)
    h3 = (r1 - h2.astype(F32)).astype(BF16)
    lc_s[...] = _dot(tri, h1) + (_dot(tri, h2) + _dot(tri, h3))
    ld_s[...] = ld
    r_s[...] = r
    k_s[...] = k
    v_s[...] = v
    kk_s[...] = kk
    b_s[...] = kk * a

    rr = lax.broadcasted_iota(jnp.int32, (2 * CHUNK, 2 * CHUNK), 0)
    cc = lax.broadcasted_iota(jnp.int32, (2 * CHUNK, 2 * CHUNK), 1)
    tt = rr % CHUNK
    ss = cc % CHUNK
    causal = ss < tt + jnp.where(rr < CHUNK, 1, 0)
    lane_z = lax.broadcasted_iota(jnp.int32, (CHUNK, 4 * HEAD), 1)
    mm = functools.partial(_mm, passes=scan_passes)

    def chunk_body(c, _):
        c0 = pl.multiple_of(c * CHUNK, CHUNK)
        sl = pl.ds(c0, CHUNK)
        lc = lc_s[sl, :]
        ldc = ld_s[sl, :]
        e_inc = jnp.exp(lc)
        e_exc = jnp.exp(lc - ldc)
        e_inv = jnp.exp(-lc)
        lc_end = lc[CHUNK - 1:CHUNK, :]
        e_end = jnp.exp(lc_end)
        e_hat = jnp.exp(lc_end - lc)
        kc = k_s[sl, :]
        bc = b_s[sl, :]
        vc = v_s[sl, :]
        rt = r_s[sl, :] * e_inc
        kap = kk_s[sl, :] * e_exc
        kt = kc * e_inv
        bt = bc * e_inv
        khat = kc * e_hat
        bhat = bc * e_hat

        heads = [slice(HEAD * h, HEAD * (h + 1)) for h in range(N_HEADS)]
        gms = []
        for hs in heads:
            lhs = jnp.concatenate([rt[:, hs], kap[:, hs]], axis=0)
            rhs = jnp.concatenate([kt[:, hs], bt[:, hs]], axis=0)
            gms.append(jnp.where(causal, mm(lhs, rhs, ((1,), (1,))), 0.0))
        zeds = []
        for hs, gm in zip(heads, gms):
            akk_v = mm(gm[CHUNK:, 0:HEAD], vc[:, hs])
            zeds.append(jnp.concatenate(
                [-gm[CHUNK:, HEAD:2 * HEAD], akk_v, kap[:, hs], jnp.zeros((CHUNK, HEAD), F32)],
                axis=1))
        for _ in range(5):
            ress = [mm(zed[:, 0:HEAD], zed) for zed in zeds]
            zeds = [jnp.where(lane_z < HEAD, res, zed + res) for zed, res in zip(zeds, ress)]
        ress = [mm(zed[:, 0:HEAD], zed) for zed in zeds]
        xsols = [zed + res for zed, res in zip(zeds, ress)]
        s_hs = [st_ref[h] for h in range(N_HEADS)]
        ds = [mm(jnp.concatenate([rt[:, hs], xsol[:, 2 * HEAD:3 * HEAD]], axis=0), s_h, ((1,), (1,)))
              for hs, xsol, s_h in zip(heads, xsols, s_hs)]
        vus = [jnp.concatenate([vc[:, hs], -(xsol[:, HEAD:2 * HEAD] + d[CHUNK:, :])], axis=0)
               for hs, xsol, d in zip(heads, xsols, ds)]
        for hs, gm, d, vu in zip(heads, gms, ds, vus):
            y_s[sl, hs] = d[:CHUNK, :] + mm(gm[:CHUNK, :], vu)
        for h, (hs, s_h, vu) in enumerate(zip(heads, s_hs, vus)):
            kb_hat = jnp.concatenate([khat[:, hs], bhat[:, hs]], axis=0)
            st_ref[h] = s_h * e_end[:, hs] + mm(vu, kb_hat, ((0,), (0,)))
        return 0

    lax.fori_loop(0, tb // CHUNK, chunk_body, 0)

    y = y_s[...]
    mu = _segsum(y, ones) * (1.0 / HEAD)
    yc = y - mu
    var = _segsum(yc * yc, ones) * (1.0 / HEAD)
    yn = yc * lax.rsqrt(var + GN_EPS) * gng_ref[...] + gnb_ref[...]
    bonus = _segsum(r_s[...] * k_s[...] * rkw_ref[...], ones) * v_s[...]
    o_ref[...] = ((yn + bonus) * g_s[...]).astype(o_ref.dtype)


def _rwkv(proj, shift0p, wkv0, p, bsz, seq, scan_passes):
    tb = 256 if seq % 256 == 0 else CHUNK
    nl = seq // tb
    t = bsz * seq

    def col(cb):
        return pl.BlockSpec((tb, D_RWKV), lambda b, l, cb=cb: (b * nl + l, cb))

    consts = [p["mu_p"], p["w0"], p["wup"], p["a0"], p["aup"], p["gup"], p["k_k"], p["k_a"],
              p["r_k"], p["gn_g"], p["gn_b"], p["tri"][:tb, :tb], p["ones"]]
    scratch = [pltpu.VMEM((1, N_SHIFT_PAD), F32)] + [pltpu.VMEM((tb, D_RWKV), F32)] * 9
    return pl.pallas_call(
        functools.partial(_rwkv_kernel, tb=tb, scan_passes=scan_passes),
        grid=(bsz, nl),
        in_specs=[col(0), col(1), col(2),
                  pl.BlockSpec((tb, N_LORA_PAD), lambda b, l: (b * nl + l, COL_LORA)),
                  pl.BlockSpec((None, 1, N_SHIFT_PAD), lambda b, l: (b, 0, 0)),
                  pl.BlockSpec((None, N_HEADS, HEAD, HEAD), lambda b, l: (b, 0, 0, 0)),
                  ] + [_const_spec(c) for c in consts],
        out_specs=[pl.BlockSpec((tb, D_RWKV), lambda b, l: (b * nl + l, 0)),
                   pl.BlockSpec((None, N_HEADS, HEAD, HEAD), lambda b, l: (b, 0, 0, 0))],
        out_shape=[jax.ShapeDtypeStruct((t, D_RWKV), BF16),
                   jax.ShapeDtypeStruct((bsz, N_HEADS, HEAD, HEAD), F32)],
        scratch_shapes=scratch,
        compiler_params=pltpu.CompilerParams(
            dimension_semantics=("parallel", "arbitrary"), vmem_limit_bytes=VMEM_LIMIT),
    )(proj, proj, proj, proj, shift0p, wkv0, *consts)


def _s5_kernel(u_ref, hre0_ref, him0_ref, lre_ref, lim_ref, wbre_ref, wbim_ref, wcre_ref,
               wcim_ref, d_ref, gluw_ref, glub_ref,
               o_ref, hre_ref, him_ref,
               xre_s, xim_s, *, tb, in_passes, out_passes):
    @pl.when(pl.program_id(1) == 0)
    def _():
        hre_ref[...] = hre0_ref[...]
        him_ref[...] = him0_ref[...]

    u = u_ref[...]
    kin = S5_BUNDLE * S5_GROUP
    kst = S5_BUNDLE * S5_STATE
    for m in range(S5_GROUPS // S5_BUNDLE):
        um = u[:, kin * m:kin * (m + 1)]
        xre_s[:, kst * m:kst * (m + 1)] = _mm(um, wbre_ref[m], passes=in_passes)
        xim_s[:, kst * m:kst * (m + 1)] = _mm(um, wbim_ref[m], passes=in_passes)

    lre = lre_ref[...]
    lim = lim_ref[...]

    def step(t, carry):
        hre, him = carry
        row = pl.ds(t, 1)
        nre = lre * hre - lim * him + xre_s[row, :]
        nim = lre * him + lim * hre + xim_s[row, :]
        xre_s[row, :] = nre
        xim_s[row, :] = nim
        return nre, nim

    hre, him = lax.fori_loop(0, tb, step, (hre_ref[...], him_ref[...]), unroll=8)
    hre_ref[...] = hre
    him_ref[...] = him

    ys = []
    for m in range(S5_GROUPS // S5_BUNDLE):
        ys.append(_mm(xre_s[:, kst * m:kst * (m + 1)], wcre_ref[m], passes=out_passes)
                  + _mm(xim_s[:, kst * m:kst * (m + 1)], wcim_ref[m], passes=out_passes))
    y = jnp.concatenate(ys, axis=1) + d_ref[...] * u
    zg = 0.5 * y * (1.0 + jnp.tanh(math.sqrt(2.0 / math.pi) * (y + 0.044715 * (y * y * y))))
    gate = jax.nn.sigmoid(_dot(zg.astype(BF16), gluw_ref[...]) + glub_ref[...])
    o_ref[...] = (zg * gate).astype(o_ref.dtype)


def _s5(proj, hre0, him0, p, bsz, seq, in_passes, out_passes):
    tb = 256 if seq % 256 == 0 else CHUNK
    nl = seq // tb
    t = bsz * seq
    consts = [p["lam_re"], p["lam_im"], p["wb_re"], p["wb_im"], p["wc_re"], p["wc_im"],
              p["s5_d"], p["glu_w"], p["glu_b"]]
    state_spec = pl.BlockSpec((None, 1, S5_WIDE), lambda b, l: (b, 0, 0))
    return pl.pallas_call(
        functools.partial(_s5_kernel, tb=tb, in_passes=in_passes, out_passes=out_passes),
        grid=(bsz, nl),
        in_specs=[pl.BlockSpec((tb, D_S5), lambda b, l: (b * nl + l, COL_U)),
                  state_spec, state_spec] + [_const_spec(c) for c in consts],
        out_specs=[pl.BlockSpec((tb, D_S5), lambda b, l: (b * nl + l, 0)), state_spec, state_spec],
        out_shape=[jax.ShapeDtypeStruct((t, D_S5), BF16),
                   jax.ShapeDtypeStruct((bsz, 1, S5_WIDE), F32),
                   jax.ShapeDtypeStruct((bsz, 1, S5_WIDE), F32)],
        scratch_shapes=[pltpu.VMEM((tb, S5_WIDE), F32)] * 2,
        compiler_params=pltpu.CompilerParams(
            dimension_semantics=("parallel", "arbitrary"), vmem_limit_bytes=VMEM_LIMIT),
    )(proj, hre0, him0, *consts)


def _out_ln_kernel(x_ref, orw_ref, os5_ref, ging_ref, binb_ref, wa_ref, wb_ref, g_ref, b_ref, o_ref):
    xn = _layer_norm(x_ref[...], ging_ref[...], binb_ref[...])
    mix = _dot(orw_ref[...], wa_ref[...]) + _dot(os5_ref[...], wb_ref[...])
    o_ref[...] = _layer_norm(ALPHA * xn + mix, g_ref[...], b_ref[...])


def _out_ln(x2d, o_rw, o_s5, p):
    t, d = x2d.shape
    tm = _row_tile(t, 512)
    consts = [p["ln_in_g"], p["ln_in_b"], p["w_out_a"], p["w_out_b"], p["ln1_g"], p["ln1_b"]]
    return pl.pallas_call(
        _out_ln_kernel,
        grid=(t // tm,),
        in_specs=[pl.BlockSpec((tm, d), lambda i: (i, 0)),
                  pl.BlockSpec((tm, D_RWKV), lambda i: (i, 0)),
                  pl.BlockSpec((tm, D_S5), lambda i: (i, 0))] + [_const_spec(c) for c in consts],
        out_specs=pl.BlockSpec((tm, d), lambda i: (i, 0)),
        out_shape=jax.ShapeDtypeStruct((t, d), F32),
        compiler_params=pltpu.CompilerParams(
            dimension_semantics=("parallel",), vmem_limit_bytes=VMEM_LIMIT),
    )(x2d, o_rw, o_s5, *consts)


def _ffn_ln_kernel(x_ref, w1_ref, w3_ref, w2_ref, g_ref, b_ref, o_ref, xb_ref, *, nf):
    f = pl.program_id(1)

    @pl.when(f == 0)
    def _():
        xb_ref[...] = x_ref[...].astype(BF16)
        o_ref[...] = jnp.zeros_like(o_ref)

    xb = xb_ref[...]
    h1 = _dot(xb, w1_ref[...])
    h3 = _dot(xb, w3_ref[...])
    act = (h1 * jax.nn.sigmoid(h1) * h3).astype(BF16)
    o_ref[...] += _dot(act, w2_ref[...])

    @pl.when(f == nf - 1)
    def _():
        o_ref[...] = _layer_norm(ALPHA * x_ref[...] + o_ref[...], g_ref[...], b_ref[...])


def _ffn_ln(x1, p):
    t, d = x1.shape
    dff = p["ffn_w1"].shape[1]
    tm = _row_tile(t, 512)
    tf = 512
    nf = dff // tf
    return pl.pallas_call(
        functools.partial(_ffn_ln_kernel, nf=nf),
        grid=(t // tm, nf),
        in_specs=[pl.BlockSpec((tm, d), lambda i, f: (i, 0)),
                  pl.BlockSpec((d, tf), lambda i, f: (0, f)),
                  pl.BlockSpec((d, tf), lambda i, f: (0, f)),
                  pl.BlockSpec((tf, d), lambda i, f: (f, 0)),
                  pl.BlockSpec((1, d), lambda i, f: (0, 0)),
                  pl.BlockSpec((1, d), lambda i, f: (0, 0))],
        out_specs=pl.BlockSpec((tm, d), lambda i, f: (i, 0)),
        out_shape=jax.ShapeDtypeStruct((t, d), F32),
        scratch_shapes=[pltpu.VMEM((tm, d), BF16)],
        compiler_params=pltpu.CompilerParams(
            dimension_semantics=("parallel", "arbitrary"), vmem_limit_bytes=VMEM_LIMIT),
    )(x1, p["ffn_w1"], p["ffn_w3"], p["ffn_w2"], p["ln2_g"], p["ln2_b"])


def _pad_cols(x, n):
    return jnp.pad(x, [(0, 0)] * (x.ndim - 1) + [(0, n - x.shape[-1])])


def _shift_layout(x):
    return jnp.concatenate([x[..., :3 * D_RWKV], _pad_cols(x[..., 3 * D_RWKV:], N_LORA_PAD)], axis=-1)


def _block_diag(blocks):
    nb, g, a, b = blocks.shape
    eye = jnp.eye(g, dtype=blocks.dtype)
    return jnp.einsum("mgab,gh->mgahb", blocks, eye).reshape(nb, g * a, g * b)


def _prepare(ln_in_g, ln_in_b, w_in, mu_shift, w0, w_lora_up, a0, a_lora_up, g_lora_up, k_k, k_a,
             r_k, gn_g, gn_b, s5_a_re, s5_a_im, s5_log_dt, s5_b_re, s5_b_im, s5_c_re, s5_c_im,
             s5_d, glu_w, glu_b, w_out, ln1_g, ln1_b, ffn_w1, ffn_w3, ffn_w2, ln2_g, ln2_b):
    row = lambda x: x.reshape(1, -1).astype(F32)
    w = w_in[0]
    o3 = 3 * D_RWKV
    w_in_p = jnp.concatenate(
        [w[:, :o3], w[:, N_SHIFT:], _pad_cols(w[:, o3:N_SHIFT], N_LORA_PAD)], axis=1).astype(BF16)
    wup = jnp.zeros((128, D_RWKV), F32).at[:LORA_W].set(w_lora_up[0]).astype(BF16)
    aup = jnp.zeros((128, D_RWKV), F32).at[LORA_W:LORA_W + LORA_A].set(a_lora_up[0]).astype(BF16)
    gup = jnp.zeros((256, D_RWKV), F32).at[:LORA_G].set(g_lora_up[0]).astype(BF16)
    idx = jnp.arange(256)
    tri = ((idx[:, None] >= idx[None, :]) & (idx[:, None] // CHUNK == idx[None, :] // CHUNK)).astype(BF16)
    ones = (idx[:, None] // HEAD == idx[None, :] // HEAD).astype(BF16)

    dt = jnp.exp(s5_log_dt[0].astype(F32))[:, None]
    a_re, a_im = s5_a_re[0].astype(F32), s5_a_im[0].astype(F32)
    mag = jnp.exp(a_re * dt)
    lam_re, lam_im = mag * jnp.cos(a_im * dt), mag * jnp.sin(a_im * dt)
    den = a_re * a_re + a_im * a_im
    q_re = ((lam_re - 1.0) * a_re + lam_im * a_im) / den
    q_im = (lam_im * a_re - (lam_re - 1.0) * a_im) / den
    b_re, b_im = s5_b_re[0].astype(F32), s5_b_im[0].astype(F32)
    bb_re = q_re[..., None] * b_re - q_im[..., None] * b_im
    bb_im = q_re[..., None] * b_im + q_im[..., None] * b_re
    nb = S5_GROUPS // S5_BUNDLE
    to_in = lambda x: _block_diag(jnp.swapaxes(x, 1, 2).reshape(nb, S5_BUNDLE, S5_GROUP, S5_STATE))
    to_out = lambda x: _block_diag(jnp.swapaxes(x, 1, 2).reshape(nb, S5_BUNDLE, S5_STATE, S5_GROUP))
    return {
        "ln_in_g": row(ln_in_g), "ln_in_b": row(ln_in_b), "w_in_p": w_in_p,
        "mu_p": _shift_layout(row(mu_shift[0])), "w0": row(w0[0]), "wup": wup, "a0": row(a0[0]),
        "aup": aup, "gup": gup, "k_k": row(k_k[0]), "k_a": row(k_a[0]), "r_k": row(r_k[0]),
        "gn_g": row(gn_g[0]), "gn_b": row(gn_b[0]), "tri": tri, "ones": ones,
        "lam_re": lam_re.reshape(1, -1), "lam_im": lam_im.reshape(1, -1),
        "wb_re": to_in(bb_re), "wb_im": to_in(bb_im),
        "wc_re": to_out(s5_c_re[0].astype(F32)), "wc_im": to_out(-s5_c_im[0].astype(F32)),
        "s5_d": row(s5_d[0]), "glu_w": glu_w[0].astype(BF16), "glu_b": row(glu_b[0]),
        "w_out_a": w_out[0, :D_RWKV].astype(BF16), "w_out_b": w_out[0, D_RWKV:].astype(BF16),
        "ln1_g": row(ln1_g[0]), "ln1_b": row(ln1_b[0]),
        "ffn_w1": ffn_w1[0].astype(BF16), "ffn_w3": ffn_w3[0].astype(BF16),
        "ffn_w2": ffn_w2[0].astype(BF16), "ln2_g": row(ln2_g[0]), "ln2_b": row(ln2_b[0]),
    }


SCAN_PASSES = 1
S5_IN_PASSES = 1
S5_OUT_PASSES = 1


def _run(x, shift, wkv, h_re, h_im, p):
    bsz, seq, d = x.shape
    x2d = x.reshape(bsz * seq, d)
    proj = _ln_proj(x2d, p["ln_in_g"], p["ln_in_b"], p["w_in_p"])
    o_rw, n_wkv = _rwkv(proj, _shift_layout(shift[0]), wkv[0], p, bsz, seq, SCAN_PASSES)
    o_s5, n_hre, n_him = _s5(proj, h_re[0].reshape(bsz, 1, S5_WIDE), h_im[0].reshape(bsz, 1, S5_WIDE),
                             p, bsz, seq, S5_IN_PASSES, S5_OUT_PASSES)
    x1 = _out_ln(x2d, o_rw, o_s5, p)
    y = _ffn_ln(x1, p).reshape(bsz, seq, d)
    last = proj.reshape(bsz, seq, N_PROJ_PAD)[:, seq - 1:, :]
    lo0 = COL_LORA * N_LORA_PAD
    n_shift = jnp.concatenate([last[..., :3 * D_RWKV], last[..., lo0:lo0 + N_LORA]], axis=-1)
    return (y, n_shift[None], n_wkv[None],
            n_hre.reshape(1, bsz, S5_GROUPS, S5_STATE), n_him.reshape(1, bsz, S5_GROUPS, S5_STATE))


def kernel(x_prompt, x_sample, cache_shift, state_wkv, state_s5_re, state_s5_im, ln_in_g, ln_in_b, w_in, mu_shift, w0, w_lora_up, a0, a_lora_up, g_lora_up, k_k, k_a, r_k, gn_g, gn_b, s5_a_re, s5_a_im, s5_log_dt, s5_b_re, s5_b_im, s5_c_re, s5_c_im, s5_d, glu_w, glu_b, w_out, ln1_g, ln1_b, ffn_w1, ffn_w3, ffn_w2, ln2_g, ln2_b):
    p = _prepare(ln_in_g, ln_in_b, w_in, mu_shift, w0, w_lora_up, a0, a_lora_up, g_lora_up, k_k,
                 k_a, r_k, gn_g, gn_b, s5_a_re, s5_a_im, s5_log_dt, s5_b_re, s5_b_im, s5_c_re,
                 s5_c_im, s5_d, glu_w, glu_b, w_out, ln1_g, ln1_b, ffn_w1, ffn_w3, ffn_w2, ln2_g,
                 ln2_b)
    bp = x_prompt.shape[0]
    dt = x_prompt.dtype
    zeros = lambda *s: jnp.zeros((1, bp) + s, dt)
    yp, sh_p, wkv_p, re_p, im_p = _run(
        x_prompt, zeros(1, N_SHIFT), zeros(N_HEADS, HEAD, HEAD), zeros(S5_GROUPS, S5_STATE),
        zeros(S5_GROUPS, S5_STATE), p)
    ys, sh_s, wkv_s, re_s, im_s = _run(x_sample, cache_shift, state_wkv, state_s5_re, state_s5_im, p)
    return (yp, ys, sh_p, wkv_p, re_p, im_p, sh_s, wkv_s, re_s, im_s)
```

```python
import functools
import math

import jax
import jax.numpy as jnp
from jax import lax
from jax.experimental import pallas as pl
from jax.experimental.pallas import tpu as pltpu

F32 = jnp.float32
BF16 = jnp.bfloat16

D_MODEL = 2048
D_RWKV = 1024
D_S5 = 1024
HEAD = 64
N_HEADS = D_RWKV // HEAD
LORA_W = 64
LORA_A = 64
LORA_G = 160
N_LORA = LORA_W + LORA_A + LORA_G
N_LORA_PAD = 512
N_SHIFT = 3 * D_RWKV + N_LORA
N_SHIFT_PAD = 3 * D_RWKV + N_LORA_PAD
N_PROJ_PAD = 3 * D_RWKV + D_S5 + N_LORA_PAD
COL_U = 3
COL_LORA = 8
S5_GROUP = 16
S5_GROUPS = D_S5 // S5_GROUP
S5_STATE = 64
S5_WIDE = S5_GROUPS * S5_STATE
S5_BUNDLE = 8
CHUNK = 64
LN_EPS = 1e-5
GN_EPS = 64e-5
ALPHA = 2.0 ** 0.25
VMEM_LIMIT = 56 * 1024 * 1024


def _split(x):
    hi = x.astype(BF16)
    lo = (x - hi.astype(F32)).astype(BF16)
    return hi, lo


def _dot(a, b, dims=((1,), (0,))):
    return lax.dot_general(a, b, (dims, ((), ())), preferred_element_type=F32)


def _mm(a, b, dims=((1,), (0,)), passes=1):
    if passes == 1:
        return _dot(a.astype(BF16), b.astype(BF16), dims)
    ah, al = _split(a)
    if passes == 2:
        bh = b.astype(BF16)
        return _dot(ah, bh, dims) + _dot(al, bh, dims)
    bh, bl = _split(b)
    return _dot(ah, bh, dims) + (_dot(al, bh, dims) + _dot(ah, bl, dims))


def _layer_norm(x, g, b):
    mu = jnp.mean(x, axis=-1, keepdims=True)
    xc = x - mu
    var = jnp.mean(xc * xc, axis=-1, keepdims=True)
    return xc * lax.rsqrt(var + LN_EPS) * g + b


def _row_tile(t, cap):
    for c in (1024, 512, 256, 128, 64):
        if c <= cap and t % c == 0:
            return c
    raise ValueError(f"token count {t} is not a multiple of 64")


def _const_spec(arr):
    nd = arr.ndim
    return pl.BlockSpec(arr.shape, lambda *_: (0,) * nd)


def _ln_proj_kernel(x_ref, g_ref, b_ref, w_ref, o_ref, xn_ref):
    @pl.when(pl.program_id(1) == 0)
    def _():
        xn_ref[...] = _layer_norm(x_ref[...], g_ref[...], b_ref[...]).astype(BF16)

    o_ref[...] = _dot(xn_ref[...], w_ref[...])


def _ln_proj(x2d, g, b, w_bf16):
    t, d = x2d.shape
    n = w_bf16.shape[1]
    tm = _row_tile(t, 1024)
    tn = n // 3
    return pl.pallas_call(
        _ln_proj_kernel,
        grid=(t // tm, n // tn),
        in_specs=[
            pl.BlockSpec((tm, d), lambda i, j: (i, 0)),
            pl.BlockSpec((1, d), lambda i, j: (0, 0)),
            pl.BlockSpec((1, d), lambda i, j: (0, 0)),
            pl.BlockSpec((d, tn), lambda i, j: (0, j)),
        ],
        out_specs=pl.BlockSpec((tm, tn), lambda i, j: (i, j)),
        out_shape=jax.ShapeDtypeStruct((t, n), F32),
        scratch_shapes=[pltpu.VMEM((tm, d), BF16)],
        compiler_params=pltpu.CompilerParams(
            dimension_semantics=("parallel", "arbitrary"), vmem_limit_bytes=VMEM_LIMIT),
    )(x2d, g, b, w_bf16)


def _segsum(x, ones):
    outs = []
    for q in range(x.shape[1] // 256):
        hi, lo = _split(x[:, 256 * q:256 * (q + 1)])
        outs.append(_dot(hi, ones) + _dot(lo, ones))
    return jnp.concatenate(outs, axis=1)


def _rwkv_kernel(pr_ref, pk_ref, pv_ref, plo_ref, sh_ref, s0_ref, mu_ref, w0_ref, wup_ref,
                 a0_ref, aup_ref, gup_ref, kkw_ref, kaw_ref, rkw_ref, gng_ref, gnb_ref,
                 tri_ref, ones_ref,
                 o_ref, st_ref,
                 carry_ref, r_s, k_s, v_s, kk_s, b_s, ld_s, lc_s, g_s, y_s,
                 *, ns, tb):
    @pl.when(pl.program_id(1) == 0)
    def _():
        carry_ref[...] = sh_ref[...]
        st_ref[...] = s0_ref[...]

    row = lax.broadcasted_iota(jnp.int32, (tb, 1), 0)

    def token_shift(x_ref, c0):
        width = x_ref.shape[-1]
        outs = []
        for s in range(ns):
            x = x_ref[s]
            prev = jnp.where(row == 0, carry_ref[s, :, c0:c0 + width], pltpu.roll(x, 1, 0))
            outs.append(x + (prev - x) * mu_ref[:, c0:c0 + width])
            carry_ref[s, :, c0:c0 + width] = x[tb - 1:tb, :]
        return jnp.concatenate(outs, axis=0)

    ones = ones_ref[...]
    r = token_shift(pr_ref, 0)
    k = token_shift(pk_ref, D_RWKV)
    v = token_shift(pv_ref, 2 * D_RWKV)
    lo = token_shift(plo_ref, 3 * D_RWKV)

    lane = lax.broadcasted_iota(jnp.int32, (ns * tb, 128), 1)
    lo_wa = lo[:, 0:128]
    act_wa = jnp.where(lane < LORA_W, jnp.tanh(lo_wa), lo_wa).astype(BF16)
    act_g = jax.nn.sigmoid(lo[:, 128:384]).astype(BF16)
    w = w0_ref[...] + _dot(act_wa, wup_ref[...])
    ld = (-math.exp(-0.5)) * jax.nn.sigmoid(w)
    a = jax.nn.sigmoid(a0_ref[...] + _dot(act_wa, aup_ref[...]))
    g_s[...] = _dot(act_g, gup_ref[...])

    kk = k * kkw_ref[...]
    kk = kk * jnp.minimum(lax.rsqrt(_segsum(kk * kk, ones)), 1e12)
    k = k * (1.0 + (a - 1.0) * kaw_ref[...])

    tri = tri_ref[...]
    h1 = ld.astype(BF16)
    r1 = ld - h1.astype(F32)
    h2 = r1.astype(BF16)
    h3 = (r1 - h2.astype(F32)).astype(BF16)
    for s in range(ns):
        rows = slice(s * tb, (s + 1) * tb)
        lc_s[rows, :] = _dot(tri, h1[rows]) + (_dot(tri, h2[rows]) + _dot(tri, h3[rows]))
    ld_s[...] = ld
    r_s[...] = r
    k_s[...] = k
    v_s[...] = v
    kk_s[...] = kk
    b_s[...] = kk * a

    rr = lax.broadcasted_iota(jnp.int32, (2 * CHUNK, 2 * CHUNK), 0)
    cc = lax.broadcasted_iota(jnp.int32, (2 * CHUNK, 2 * CHUNK), 1)
    causal = cc % CHUNK < rr % CHUNK + jnp.where(rr < CHUNK, 1, 0)
    t_i = lax.broadcasted_iota(jnp.int32, (CHUNK, CHUNK), 0)
    s_i = lax.broadcasted_iota(jnp.int32, (CHUNK, CHUNK), 1)

    def same_block(size):
        return t_i // size == s_i // size

    diag_mask = same_block(_SOLVE_BASE)
    merge_masks = []
    size = _SOLVE_BASE
    while size < CHUNK:
        merge_masks.append(same_block(2 * size) & jnp.logical_not(same_block(size)))
        size *= 2
    lane_z = lax.broadcasted_iota(jnp.int32, (CHUNK, 4 * HEAD), 1)
    eye = jnp.where(t_i == s_i, 1.0, 0.0).astype(F32)

    def chunk_body(c, _):
        c0 = pl.multiple_of(c * CHUNK, CHUNK)
        units = []
        per_seq = []
        for s in range(ns):
            sl = pl.ds(s * tb + c0, CHUNK)
            lc = lc_s[sl, :]
            e_inc = jnp.exp(lc)
            e_exc = jnp.exp(lc - ld_s[sl, :])
            e_inv = jnp.exp(-lc)
            lc_end = lc[CHUNK - 1:CHUNK, :]
            e_hat = jnp.exp(lc_end - lc)
            kc = k_s[sl, :]
            bc = b_s[sl, :]
            per_seq.append(dict(
                sl=sl, e_end=jnp.exp(lc_end), vc=v_s[sl, :], rt=r_s[sl, :] * e_inc,
                kap=kk_s[sl, :] * e_exc, kt=kc * e_inv, bt=bc * e_inv, khat=kc * e_hat,
                bhat=bc * e_hat))
            units += [(s, h, slice(HEAD * h, HEAD * (h + 1))) for h in range(N_HEADS)]

        gms = []
        for s, h, hs in units:
            q = per_seq[s]
            lhs = jnp.concatenate([q["rt"][:, hs], q["kap"][:, hs]], axis=0)
            rhs = jnp.concatenate([q["kt"][:, hs], q["bt"][:, hs]], axis=0)
            gms.append(jnp.where(causal, _mm(lhs, rhs, ((1,), (1,))), 0.0))
        nmats = [-gm[CHUNK:, HEAD:2 * HEAD] for gm in gms]
        zeds = []
        for (s, h, hs), gm, nm in zip(units, gms, nmats):
            q = per_seq[s]
            akk_v = _mm(gm[CHUNK:, 0:HEAD], q["vc"][:, hs])
            zeds.append(jnp.concatenate(
                [jnp.where(diag_mask, nm, 0.0), eye, akk_v, q["kap"][:, hs]], axis=1))
        steps = _SOLVE_BASE.bit_length() - 1
        for it in range(steps):
            ress = [_mm(zed[:, 0:HEAD], zed) for zed in zeds]
            if it < steps - 1:
                zeds = [jnp.where(lane_z < HEAD, res, zed + res) for zed, res in zip(zeds, ress)]
            else:
                zeds = [zed + res for zed, res in zip(zeds, ress)]
        for mask in merge_masks:
            ress = [_mm(jnp.where(mask, nm, 0.0), zed) for nm, zed in zip(nmats, zeds)]
            ress = [_mm(zed[:, HEAD:2 * HEAD], res) for zed, res in zip(zeds, ress)]
            zeds = [zed + res for zed, res in zip(zeds, ress)]
        s_hs = [st_ref[s, h] for s, h, _ in units]
        ds = [_mm(jnp.concatenate([per_seq[s]["rt"][:, hs], zed[:, 3 * HEAD:4 * HEAD]], axis=0),
                  s_h, ((1,), (1,)))
              for (s, h, hs), zed, s_h in zip(units, zeds, s_hs)]
        vus = [jnp.concatenate([per_seq[s]["vc"][:, hs],
                                -(zed[:, 2 * HEAD:3 * HEAD] + d[CHUNK:, :])], axis=0)
               for (s, h, hs), zed, d in zip(units, zeds, ds)]
        for (s, h, hs), gm, d, vu in zip(units, gms, ds, vus):
            y_s[per_seq[s]["sl"], hs] = d[:CHUNK, :] + _mm(gm[:CHUNK, :], vu)
        for (s, h, hs), s_h, vu in zip(units, s_hs, vus):
            q = per_seq[s]
            kb_hat = jnp.concatenate([q["khat"][:, hs], q["bhat"][:, hs]], axis=0)
            st_ref[s, h] = s_h * q["e_end"][:, hs] + _mm(vu, kb_hat, ((0,), (0,)))
        return 0

    lax.fori_loop(0, tb // CHUNK, chunk_body, 0)

    y = y_s[...]
    mu = _segsum(y, ones) * (1.0 / HEAD)
    yc = y - mu
    var = _segsum(yc * yc, ones) * (1.0 / HEAD)
    yn = yc * lax.rsqrt(var + GN_EPS) * gng_ref[...] + gnb_ref[...]
    bonus = _segsum(r_s[...] * k_s[...] * rkw_ref[...], ones) * v_s[...]
    o_ref[...] = ((yn + bonus) * g_s[...]).astype(o_ref.dtype).reshape(ns, tb, D_RWKV)


_SOLVE_BASE = 8
_RWKV_SEQS = 2


def _rwkv(proj3, shift0p, wkv0, p):
    bsz, seq, _ = proj3.shape
    ns = _RWKV_SEQS
    assert bsz % ns == 0
    tb = 256 if seq % 256 == 0 else CHUNK
    nl = seq // tb

    def col(cb, width=D_RWKV):
        return pl.BlockSpec((ns, tb, width), lambda b, l, cb=cb: (b, l, cb))

    consts = [p["mu_p"], p["w0"], p["wup"], p["a0"], p["aup"], p["gup"], p["k_k"], p["k_a"],
              p["r_k"], p["gn_g"], p["gn_b"], p["tri"][:tb, :tb], p["ones"]]
    state_spec = pl.BlockSpec((ns, N_HEADS, HEAD, HEAD), lambda b, l: (b, 0, 0, 0))
    scratch = ([pltpu.VMEM((ns, 1, N_SHIFT_PAD), F32)]
               + [pltpu.VMEM((ns * tb, D_RWKV), F32)] * 9)
    return pl.pallas_call(
        functools.partial(_rwkv_kernel, ns=ns, tb=tb),
        grid=(bsz // ns, nl),
        in_specs=[col(0), col(1), col(2), col(COL_LORA, N_LORA_PAD),
                  pl.BlockSpec((ns, 1, N_SHIFT_PAD), lambda b, l: (b, 0, 0)),
                  state_spec] + [_const_spec(c) for c in consts],
        out_specs=[pl.BlockSpec((ns, tb, D_RWKV), lambda b, l: (b, l, 0)), state_spec],
        out_shape=[jax.ShapeDtypeStruct((bsz, seq, D_RWKV), BF16),
                   jax.ShapeDtypeStruct((bsz, N_HEADS, HEAD, HEAD), F32)],
        scratch_shapes=scratch,
        compiler_params=pltpu.CompilerParams(
            dimension_semantics=("parallel", "arbitrary"), vmem_limit_bytes=VMEM_LIMIT),
    )(proj3, proj3, proj3, proj3, shift0p, wkv0, *consts)


def _s5_kernel(u_ref, hre0_ref, him0_ref, lre_ref, lim_ref, wbre_ref, wbim_ref, wcre_ref,
               wcim_ref, d_ref, gluw_ref, glub_ref,
               o_ref, hre_ref, him_ref,
               xre_s, xim_s, hre_s, him_s, *, ns, tb):
    @pl.when(pl.program_id(1) == 0)
    def _():
        hre_ref[...] = hre0_ref[...]
        him_ref[...] = him0_ref[...]

    u = u_ref[...].reshape(ns * tb, D_S5)
    kin = S5_BUNDLE * S5_GROUP
    kst = S5_BUNDLE * S5_STATE
    for m in range(S5_GROUPS // S5_BUNDLE):
        um = u[:, kin * m:kin * (m + 1)].astype(BF16)
        xre_s[:, kst * m:kst * (m + 1)] = _dot(um, wbre_ref[m])
        xim_s[:, kst * m:kst * (m + 1)] = _dot(um, wbim_ref[m])

    lre = lre_ref[...]
    lim = lim_ref[...]

    def step(t, carry):
        out = []
        for s in range(ns):
            hre, him = carry[2 * s], carry[2 * s + 1]
            row = pl.ds(s * tb + t, 1)
            nre = lre * hre - lim * him + xre_s[row, :]
            nim = lre * him + lim * hre + xim_s[row, :]
            hre_s[row, :] = nre
            him_s[row, :] = nim
            out += [nre, nim]
        return tuple(out)

    init = []
    for s in range(ns):
        init += [hre_ref[s], him_ref[s]]
    fin = lax.fori_loop(0, tb, step, tuple(init), unroll=8)
    for s in range(ns):
        hre_ref[s] = fin[2 * s]
        him_ref[s] = fin[2 * s + 1]

    ys = []
    for m in range(S5_GROUPS // S5_BUNDLE):
        ys.append(_dot(hre_s[:, kst * m:kst * (m + 1)].astype(BF16), wcre_ref[m])
                  + _dot(him_s[:, kst * m:kst * (m + 1)].astype(BF16), wcim_ref[m]))
    y = jnp.concatenate(ys, axis=1) + d_ref[...] * u
    zg = 0.5 * y * (1.0 + jnp.tanh(math.sqrt(2.0 / math.pi) * (y + 0.044715 * (y * y * y))))
    gate = jax.nn.sigmoid(_dot(zg.astype(BF16), gluw_ref[...]) + glub_ref[...])
    o_ref[...] = (zg * gate).astype(o_ref.dtype).reshape(ns, tb, D_S5)


_S5_SEQS = 2


def _s5(proj3, hre0, him0, p):
    bsz, seq, _ = proj3.shape
    ns = _S5_SEQS
    assert bsz % ns == 0
    tb = 128 if seq % 128 == 0 else CHUNK
    nl = seq // tb
    consts = [p["lam_re"], p["lam_im"], p["wb_re"], p["wb_im"], p["wc_re"], p["wc_im"],
              p["s5_d"], p["glu_w"], p["glu_b"]]
    state_spec = pl.BlockSpec((ns, 1, S5_WIDE), lambda b, l: (b, 0, 0))
    return pl.pallas_call(
        functools.partial(_s5_kernel, ns=ns, tb=tb),
        grid=(bsz // ns, nl),
        in_specs=[pl.BlockSpec((ns, tb, D_S5), lambda b, l: (b, l, COL_U)),
                  state_spec, state_spec] + [_const_spec(c) for c in consts],
        out_specs=[pl.BlockSpec((ns, tb, D_S5), lambda b, l: (b, l, 0)), state_spec, state_spec],
        out_shape=[jax.ShapeDtypeStruct((bsz, seq, D_S5), BF16),
                   jax.ShapeDtypeStruct((bsz, 1, S5_WIDE), F32),
                   jax.ShapeDtypeStruct((bsz, 1, S5_WIDE), F32)],
        scratch_shapes=[pltpu.VMEM((ns * tb, S5_WIDE), F32)] * 4,
        compiler_params=pltpu.CompilerParams(
            dimension_semantics=("parallel", "arbitrary"), vmem_limit_bytes=VMEM_LIMIT),
    )(proj3, hre0, him0, *consts)


def _out_ln_kernel(x_ref, orw_ref, os5_ref, ging_ref, binb_ref, wa_ref, wb_ref, g_ref, b_ref, o_ref):
    xn = _layer_norm(x_ref[...], ging_ref[...], binb_ref[...])
    mix = _dot(orw_ref[...], wa_ref[...]) + _dot(os5_ref[...], wb_ref[...])
    o_ref[...] = _layer_norm(ALPHA * xn + mix, g_ref[...], b_ref[...])


def _out_ln(x2d, o_rw, o_s5, p):
    t, d = x2d.shape
    tm = _row_tile(t, 512)
    consts = [p["ln_in_g"], p["ln_in_b"], p["w_out_a"], p["w_out_b"], p["ln1_g"], p["ln1_b"]]
    return pl.pallas_call(
        _out_ln_kernel,
        grid=(t // tm,),
        in_specs=[pl.BlockSpec((tm, d), lambda i: (i, 0)),
                  pl.BlockSpec((tm, D_RWKV), lambda i: (i, 0)),
                  pl.BlockSpec((tm, D_S5), lambda i: (i, 0))] + [_const_spec(c) for c in consts],
        out_specs=pl.BlockSpec((tm, d), lambda i: (i, 0)),
        out_shape=jax.ShapeDtypeStruct((t, d), F32),
        compiler_params=pltpu.CompilerParams(
            dimension_semantics=("parallel",), vmem_limit_bytes=VMEM_LIMIT),
    )(x2d, o_rw, o_s5, *consts)


def _ffn_ln_kernel(x_ref, w1_ref, w3_ref, w2_ref, g_ref, b_ref, o_ref, xb_ref, *, nf):
    f = pl.program_id(1)

    @pl.when(f == 0)
    def _():
        xb_ref[...] = x_ref[...].astype(BF16)

    xb = xb_ref[...]
    h1 = _dot(xb, w1_ref[...])
    h3 = _dot(xb, w3_ref[...])
    act = (h1 * jax.nn.sigmoid(h1) * h3).astype(BF16)
    part = _dot(act, w2_ref[...])

    @pl.when(f == 0)
    def _():
        o_ref[...] = part

    @pl.when(f > 0)
    def _():
        o_ref[...] += part

    @pl.when(f == nf - 1)
    def _():
        o_ref[...] = _layer_norm(ALPHA * x_ref[...] + o_ref[...], g_ref[...], b_ref[...])


def _ffn_ln(x1, p):
    t, d = x1.shape
    dff = p["ffn_w1"].shape[1]
    tm = _row_tile(t, 512)
    tf = 512
    nf = dff // tf
    return pl.pallas_call(
        functools.partial(_ffn_ln_kernel, nf=nf),
        grid=(t // tm, nf),
        in_specs=[pl.BlockSpec((tm, d), lambda i, f: (i, 0)),
                  pl.BlockSpec((d, tf), lambda i, f: (0, f)),
                  pl.BlockSpec((d, tf), lambda i, f: (0, f)),
                  pl.BlockSpec((tf, d), lambda i, f: (f, 0)),
                  pl.BlockSpec((1, d), lambda i, f: (0, 0)),
                  pl.BlockSpec((1, d), lambda i, f: (0, 0))],
        out_specs=pl.BlockSpec((tm, d), lambda i, f: (i, 0)),
        out_shape=jax.ShapeDtypeStruct((t, d), F32),
        scratch_shapes=[pltpu.VMEM((tm, d), BF16)],
        compiler_params=pltpu.CompilerParams(
            dimension_semantics=("parallel", "arbitrary"), vmem_limit_bytes=VMEM_LIMIT),
    )(x1, p["ffn_w1"], p["ffn_w3"], p["ffn_w2"], p["ln2_g"], p["ln2_b"])


def _pad_cols(x, n):
    return jnp.pad(x, [(0, 0)] * (x.ndim - 1) + [(0, n - x.shape[-1])])


def _shift_layout(x):
    return jnp.concatenate([x[..., :3 * D_RWKV], _pad_cols(x[..., 3 * D_RWKV:], N_LORA_PAD)], axis=-1)


def _block_diag(blocks):
    nb, g, a, b = blocks.shape
    eye = jnp.eye(g, dtype=blocks.dtype)
    return jnp.einsum("mgab,gh->mgahb", blocks, eye).reshape(nb, g * a, g * b)


def _prepare(ln_in_g, ln_in_b, w_in, mu_shift, w0, w_lora_up, a0, a_lora_up, g_lora_up, k_k, k_a,
             r_k, gn_g, gn_b, s5_a_re, s5_a_im, s5_log_dt, s5_b_re, s5_b_im, s5_c_re, s5_c_im,
             s5_d, glu_w, glu_b, w_out, ln1_g, ln1_b, ffn_w1, ffn_w3, ffn_w2, ln2_g, ln2_b):
    row = lambda x: x.reshape(1, -1).astype(F32)
    w = w_in[0]
    o3 = 3 * D_RWKV
    w_in_p = jnp.concatenate(
        [w[:, :o3], w[:, N_SHIFT:], _pad_cols(w[:, o3:N_SHIFT], N_LORA_PAD)], axis=1).astype(BF16)
    wup = jnp.zeros((128, D_RWKV), F32).at[:LORA_W].set(w_lora_up[0]).astype(BF16)
    aup = jnp.zeros((128, D_RWKV), F32).at[LORA_W:LORA_W + LORA_A].set(a_lora_up[0]).astype(BF16)
    gup = jnp.zeros((256, D_RWKV), F32).at[:LORA_G].set(g_lora_up[0]).astype(BF16)
    idx = jnp.arange(256)
    tri = ((idx[:, None] >= idx[None, :]) & (idx[:, None] // CHUNK == idx[None, :] // CHUNK)).astype(BF16)
    ones = (idx[:, None] // HEAD == idx[None, :] // HEAD).astype(BF16)

    dt = jnp.exp(s5_log_dt[0].astype(F32))[:, None]
    a_re, a_im = s5_a_re[0].astype(F32), s5_a_im[0].astype(F32)
    mag = jnp.exp(a_re * dt)
    lam_re, lam_im = mag * jnp.cos(a_im * dt), mag * jnp.sin(a_im * dt)
    den = a_re * a_re + a_im * a_im
    q_re = ((lam_re - 1.0) * a_re + lam_im * a_im) / den
    q_im = (lam_im * a_re - (lam_re - 1.0) * a_im) / den
    b_re, b_im = s5_b_re[0].astype(F32), s5_b_im[0].astype(F32)
    bb_re = q_re[..., None] * b_re - q_im[..., None] * b_im
    bb_im = q_re[..., None] * b_im + q_im[..., None] * b_re
    nb = S5_GROUPS // S5_BUNDLE
    to_in = lambda x: _block_diag(jnp.swapaxes(x, 1, 2).reshape(nb, S5_BUNDLE, S5_GROUP, S5_STATE))
    to_out = lambda x: _block_diag(jnp.swapaxes(x, 1, 2).reshape(nb, S5_BUNDLE, S5_STATE, S5_GROUP))
    return {
        "ln_in_g": row(ln_in_g), "ln_in_b": row(ln_in_b), "w_in_p": w_in_p,
        "mu_p": _shift_layout(row(mu_shift[0])), "w0": row(w0[0]), "wup": wup, "a0": row(a0[0]),
        "aup": aup, "gup": gup, "k_k": row(k_k[0]), "k_a": row(k_a[0]), "r_k": row(r_k[0]),
        "gn_g": row(gn_g[0]), "gn_b": row(gn_b[0]), "tri": tri, "ones": ones,
        "lam_re": lam_re.reshape(1, -1), "lam_im": lam_im.reshape(1, -1),
        "wb_re": to_in(bb_re).astype(BF16), "wb_im": to_in(bb_im).astype(BF16),
        "wc_re": to_out(s5_c_re[0].astype(F32)).astype(BF16),
        "wc_im": to_out(-s5_c_im[0].astype(F32)).astype(BF16),
        "s5_d": row(s5_d[0]), "glu_w": glu_w[0].astype(BF16), "glu_b": row(glu_b[0]),
        "w_out_a": w_out[0, :D_RWKV].astype(BF16), "w_out_b": w_out[0, D_RWKV:].astype(BF16),
        "ln1_g": row(ln1_g[0]), "ln1_b": row(ln1_b[0]),
        "ffn_w1": ffn_w1[0].astype(BF16), "ffn_w3": ffn_w3[0].astype(BF16),
        "ffn_w2": ffn_w2[0].astype(BF16), "ln2_g": row(ln2_g[0]), "ln2_b": row(ln2_b[0]),
    }


def _run(x, shift, wkv, h_re, h_im, p):
    bsz, seq, d = x.shape
    x2d = x.reshape(bsz * seq, d)
    proj = _ln_proj(x2d, p["ln_in_g"], p["ln_in_b"], p["w_in_p"])
    proj3 = proj.reshape(bsz, seq, N_PROJ_PAD)
    o_rw, n_wkv = _rwkv(proj3, _shift_layout(shift[0]), wkv[0], p)
    o_s5, n_hre, n_him = _s5(proj3, h_re[0].reshape(bsz, 1, S5_WIDE),
                             h_im[0].reshape(bsz, 1, S5_WIDE), p)
    x1 = _out_ln(x2d, o_rw.reshape(bsz * seq, D_RWKV), o_s5.reshape(bsz * seq, D_S5), p)
    y = _ffn_ln(x1, p).reshape(bsz, seq, d)
    last = proj3[:, seq - 1:, :]
    lo0 = COL_LORA * N_LORA_PAD
    n_shift = jnp.concatenate([last[..., :3 * D_RWKV], last[..., lo0:lo0 + N_LORA]], axis=-1)
    return (y, n_shift[None], n_wkv[None],
            n_hre.reshape(1, bsz, S5_GROUPS, S5_STATE), n_him.reshape(1, bsz, S5_GROUPS, S5_STATE))


def kernel(x_prompt, x_sample, cache_shift, state_wkv, state_s5_re, state_s5_im, ln_in_g, ln_in_b, w_in, mu_shift, w0, w_lora_up, a0, a_lora_up, g_lora_up, k_k, k_a, r_k, gn_g, gn_b, s5_a_re, s5_a_im, s5_log_dt, s5_b_re, s5_b_im, s5_c_re, s5_c_im, s5_d, glu_w, glu_b, w_out, ln1_g, ln1_b, ffn_w1, ffn_w3, ffn_w2, ln2_g, ln2_b):
    p = _prepare(ln_in_g, ln_in_b, w_in, mu_shift, w0, w_lora_up, a0, a_lora_up, g_lora_up, k_k,
                 k_a, r_k, gn_g, gn_b, s5_a_re, s5_a_im, s5_log_dt, s5_b_re, s5_b_im, s5_c_re,
                 s5_c_im, s5_d, glu_w, glu_b, w_out, ln1_g, ln1_b, ffn_w1, ffn_w3, ffn_w2, ln2_g,
                 ln2_b)
    bp = x_prompt.shape[0]
    dt = x_prompt.dtype
    zeros = lambda *s: jnp.zeros((1, bp) + s, dt)
    yp, sh_p, wkv_p, re_p, im_p = _run(
        x_prompt, zeros(1, N_SHIFT), zeros(N_HEADS, HEAD, HEAD), zeros(S5_GROUPS, S5_STATE),
        zeros(S5_GROUPS, S5_STATE), p)
    ys, sh_s, wkv_s, re_s, im_s = _run(x_sample, cache_shift, state_wkv, state_s5_re, state_s5_im, p)
    return (yp, ys, sh_p, wkv_p, re_p, im_p, sh_s, wkv_s, re_s, im_s)
```

```python
import functools
import math

import jax
import jax.numpy as jnp
from jax import lax
from jax.experimental import pallas as pl
from jax.experimental.pallas import tpu as pltpu

F32 = jnp.float32
BF16 = jnp.bfloat16

D_MODEL = 2048
D_RWKV = 1024
D_S5 = 1024
HEAD = 64
N_HEADS = D_RWKV // HEAD
LORA_W = 64
LORA_A = 64
LORA_G = 160
N_LORA = LORA_W + LORA_A + LORA_G
N_LORA_PAD = 512
N_SHIFT = 3 * D_RWKV + N_LORA
N_SHIFT_PAD = 3 * D_RWKV + N_LORA_PAD
N_PROJ_PAD = 3 * D_RWKV + D_S5 + N_LORA_PAD
COL_U = 3
COL_LORA = 8
S5_GROUP = 16
S5_GROUPS = D_S5 // S5_GROUP
S5_STATE = 64
S5_WIDE = S5_GROUPS * S5_STATE
S5_BUNDLE = 8
CHUNK = 64
LN_EPS = 1e-5
GN_EPS = 64e-5
ALPHA = 2.0 ** 0.25
VMEM_LIMIT = 56 * 1024 * 1024


def _split(x):
    hi = x.astype(BF16)
    lo = (x - hi.astype(F32)).astype(BF16)
    return hi, lo


def _dot(a, b, dims=((1,), (0,))):
    return lax.dot_general(a, b, (dims, ((), ())), preferred_element_type=F32)


def _mm(a, b, dims=((1,), (0,)), passes=1):
    if passes == 1:
        return _dot(a.astype(BF16), b.astype(BF16), dims)
    ah, al = _split(a)
    if passes == 2:
        bh = b.astype(BF16)
        return _dot(ah, bh, dims) + _dot(al, bh, dims)
    bh, bl = _split(b)
    return _dot(ah, bh, dims) + (_dot(al, bh, dims) + _dot(ah, bl, dims))


def _layer_norm(x, g, b):
    mu = jnp.mean(x, axis=-1, keepdims=True)
    xc = x - mu
    var = jnp.mean(xc * xc, axis=-1, keepdims=True)
    return xc * lax.rsqrt(var + LN_EPS) * g + b


def _row_tile(t, cap):
    for c in (1024, 768, 512, 256, 128, 64):
        if c <= cap and t % c == 0:
            return c
    raise ValueError(f"token count {t} is not a multiple of 64")


def _const_spec(arr):
    nd = arr.ndim
    return pl.BlockSpec(arr.shape, lambda *_: (0,) * nd)


def _ln_proj_kernel(x_ref, g_ref, b_ref, w_ref, o_ref, xn_ref):
    @pl.when(pl.program_id(1) == 0)
    def _():
        xn_ref[...] = _layer_norm(x_ref[...], g_ref[...], b_ref[...]).astype(BF16)

    o_ref[...] = _dot(xn_ref[...], w_ref[...])


def _ln_proj(x2d, g, b, w_bf16):
    t, d = x2d.shape
    n = w_bf16.shape[1]
    tm = _row_tile(t, 1024)
    tn = n // 3
    return pl.pallas_call(
        _ln_proj_kernel,
        grid=(t // tm, n // tn),
        in_specs=[
            pl.BlockSpec((tm, d), lambda i, j: (i, 0)),
            pl.BlockSpec((1, d), lambda i, j: (0, 0)),
            pl.BlockSpec((1, d), lambda i, j: (0, 0)),
            pl.BlockSpec((d, tn), lambda i, j: (0, j)),
        ],
        out_specs=pl.BlockSpec((tm, tn), lambda i, j: (i, j)),
        out_shape=jax.ShapeDtypeStruct((t, n), F32),
        scratch_shapes=[pltpu.VMEM((tm, d), BF16)],
        compiler_params=pltpu.CompilerParams(
            dimension_semantics=("parallel", "arbitrary"), vmem_limit_bytes=VMEM_LIMIT),
    )(x2d, g, b, w_bf16)


def _segsum(x, ones):
    outs = []
    for q in range(x.shape[1] // 256):
        hi, lo = _split(x[:, 256 * q:256 * (q + 1)])
        outs.append(_dot(hi, ones) + _dot(lo, ones))
    return jnp.concatenate(outs, axis=1)


def _rwkv_kernel(pr_ref, pk_ref, pv_ref, plo_ref, sh_ref, s0_ref, mu_ref, w0_ref, wup_ref,
                 a0_ref, aup_ref, gup_ref, kkw_ref, kaw_ref, rkw_ref, gng_ref, gnb_ref,
                 tri_ref, ones_ref,
                 o_ref, st_ref,
                 carry_ref, r_s, k_s, v_s, kk_s, b_s, ld_s, lc_s, g_s, y_s,
                 *, ns, tb):
    @pl.when(pl.program_id(1) == 0)
    def _():
        carry_ref[...] = sh_ref[...]
        st_ref[...] = s0_ref[...]

    row = lax.broadcasted_iota(jnp.int32, (tb, 1), 0)

    def token_shift(x_ref, c0):
        width = x_ref.shape[-1]
        outs = []
        for s in range(ns):
            x = x_ref[s]
            prev = jnp.where(row == 0, carry_ref[s, :, c0:c0 + width], pltpu.roll(x, 1, 0))
            outs.append(x + (prev - x) * mu_ref[:, c0:c0 + width])
            carry_ref[s, :, c0:c0 + width] = x[tb - 1:tb, :]
        return jnp.concatenate(outs, axis=0)

    ones = ones_ref[...]
    r = token_shift(pr_ref, 0)
    k = token_shift(pk_ref, D_RWKV)
    v = token_shift(pv_ref, 2 * D_RWKV)
    lo = token_shift(plo_ref, 3 * D_RWKV)

    lane = lax.broadcasted_iota(jnp.int32, (ns * tb, 128), 1)
    lo_wa = lo[:, 0:128]
    act_wa = jnp.where(lane < LORA_W, jnp.tanh(lo_wa), lo_wa).astype(BF16)
    act_g = jax.nn.sigmoid(lo[:, 128:384]).astype(BF16)
    w = w0_ref[...] + _dot(act_wa, wup_ref[...])
    ld = (-math.exp(-0.5)) * jax.nn.sigmoid(w)
    a = jax.nn.sigmoid(a0_ref[...] + _dot(act_wa, aup_ref[...]))
    g_s[...] = _dot(act_g, gup_ref[...])

    kk = k * kkw_ref[...]
    kk = kk * jnp.minimum(lax.rsqrt(_segsum(kk * kk, ones)), 1e12)
    k = k * (1.0 + (a - 1.0) * kaw_ref[...])

    tri = tri_ref[...]
    h1 = ld.astype(BF16)
    r1 = ld - h1.astype(F32)
    h2 = r1.astype(BF16)
    h3 = (r1 - h2.astype(F32)).astype(BF16)
    for s in range(ns):
        rows = slice(s * tb, (s + 1) * tb)
        lc_s[rows, :] = _dot(tri, h1[rows]) + (_dot(tri, h2[rows]) + _dot(tri, h3[rows]))
    ld_s[...] = ld
    r_s[...] = r
    k_s[...] = k
    v_s[...] = v
    kk_s[...] = kk
    b_s[...] = kk * a

    rr = lax.broadcasted_iota(jnp.int32, (2 * CHUNK, 2 * CHUNK), 0)
    cc = lax.broadcasted_iota(jnp.int32, (2 * CHUNK, 2 * CHUNK), 1)
    causal = cc % CHUNK < rr % CHUNK + jnp.where(rr < CHUNK, 1, 0)
    t_i = lax.broadcasted_iota(jnp.int32, (CHUNK, CHUNK), 0)
    s_i = lax.broadcasted_iota(jnp.int32, (CHUNK, CHUNK), 1)

    def same_block(size):
        return t_i // size == s_i // size

    diag_mask = same_block(_SOLVE_BASE)
    merge_masks = []
    size = _SOLVE_BASE
    while size < CHUNK:
        merge_masks.append(same_block(2 * size) & jnp.logical_not(same_block(size)))
        size *= 2
    lane_z = lax.broadcasted_iota(jnp.int32, (CHUNK, 4 * HEAD), 1)
    eye = jnp.where(t_i == s_i, 1.0, 0.0).astype(F32)

    def chunk_body(c, _):
        c0 = pl.multiple_of(c * CHUNK, CHUNK)
        units = []
        per_seq = []
        for s in range(ns):
            sl = pl.ds(s * tb + c0, CHUNK)
            lc = lc_s[sl, :]
            e_inc = jnp.exp(lc)
            e_exc = jnp.exp(lc - ld_s[sl, :])
            e_inv = jnp.exp(-lc)
            lc_end = lc[CHUNK - 1:CHUNK, :]
            e_hat = jnp.exp(lc_end - lc)
            kc = k_s[sl, :]
            bc = b_s[sl, :]
            per_seq.append(dict(
                sl=sl, e_end=jnp.exp(lc_end), vc=v_s[sl, :], rt=r_s[sl, :] * e_inc,
                kap=kk_s[sl, :] * e_exc, kt=kc * e_inv, bt=bc * e_inv, khat=kc * e_hat,
                bhat=bc * e_hat))
            units += [(s, h, slice(HEAD * h, HEAD * (h + 1))) for h in range(N_HEADS)]

        gms = []
        for s, h, hs in units:
            q = per_seq[s]
            lhs = jnp.concatenate([q["rt"][:, hs], q["kap"][:, hs]], axis=0)
            rhs = jnp.concatenate([q["kt"][:, hs], q["bt"][:, hs]], axis=0)
            gms.append(jnp.where(causal, _mm(lhs, rhs, ((1,), (1,))), 0.0))
        nmats = [-gm[CHUNK:, HEAD:2 * HEAD] for gm in gms]
        zeds = []
        for (s, h, hs), gm, nm in zip(units, gms, nmats):
            q = per_seq[s]
            akk_v = _mm(gm[CHUNK:, 0:HEAD], q["vc"][:, hs])
            zeds.append(jnp.concatenate(
                [jnp.where(diag_mask, nm, 0.0), eye, akk_v, q["kap"][:, hs]], axis=1))
        steps = _SOLVE_BASE.bit_length() - 1
        for it in range(steps):
            ress = [_mm(zed[:, 0:HEAD], zed) for zed in zeds]
            if it < steps - 1:
                zeds = [jnp.where(lane_z < HEAD, res, zed + res) for zed, res in zip(zeds, ress)]
            else:
                zeds = [zed + res for zed, res in zip(zeds, ress)]
        for mask in merge_masks:
            ress = [_mm(jnp.where(mask, nm, 0.0), zed) for nm, zed in zip(nmats, zeds)]
            ress = [_mm(zed[:, HEAD:2 * HEAD], res) for zed, res in zip(zeds, ress)]
            zeds = [zed + res for zed, res in zip(zeds, ress)]
        s_hs = [st_ref[s, h] for s, h, _ in units]
        ds = [_mm(jnp.concatenate([per_seq[s]["rt"][:, hs], zed[:, 3 * HEAD:4 * HEAD]], axis=0),
                  s_h, ((1,), (1,)))
              for (s, h, hs), zed, s_h in zip(units, zeds, s_hs)]
        vus = [jnp.concatenate([per_seq[s]["vc"][:, hs],
                                -(zed[:, 2 * HEAD:3 * HEAD] + d[CHUNK:, :])], axis=0)
               for (s, h, hs), zed, d in zip(units, zeds, ds)]
        for (s, h, hs), gm, d, vu in zip(units, gms, ds, vus):
            y_s[per_seq[s]["sl"], hs] = d[:CHUNK, :] + _mm(gm[:CHUNK, :], vu)
        for (s, h, hs), s_h, vu in zip(units, s_hs, vus):
            q = per_seq[s]
            kb_hat = jnp.concatenate([q["khat"][:, hs], q["bhat"][:, hs]], axis=0)
            st_ref[s, h] = s_h * q["e_end"][:, hs] + _mm(vu, kb_hat, ((0,), (0,)))
        return 0

    lax.fori_loop(0, tb // CHUNK, chunk_body, 0)

    y = y_s[...]
    mu = _segsum(y, ones) * (1.0 / HEAD)
    yc = y - mu
    var = _segsum(yc * yc, ones) * (1.0 / HEAD)
    yn = yc * lax.rsqrt(var + GN_EPS) * gng_ref[...] + gnb_ref[...]
    bonus = _segsum(r_s[...] * k_s[...] * rkw_ref[...], ones) * v_s[...]
    o_ref[...] = ((yn + bonus) * g_s[...]).astype(o_ref.dtype).reshape(ns, tb, D_RWKV)


_SOLVE_BASE = 8
_RWKV_SEQS = 2


def _rwkv(proj3, shift0p, wkv0, p):
    bsz, seq, _ = proj3.shape
    ns = _RWKV_SEQS
    assert bsz % ns == 0
    tb = 256 if seq % 256 == 0 else CHUNK
    nl = seq // tb

    def col(cb, width=D_RWKV):
        return pl.BlockSpec((ns, tb, width), lambda b, l, cb=cb: (b, l, cb))

    consts = [p["mu_p"], p["w0"], p["wup"], p["a0"], p["aup"], p["gup"], p["k_k"], p["k_a"],
              p["r_k"], p["gn_g"], p["gn_b"], p["tri"][:tb, :tb], p["ones"]]
    state_spec = pl.BlockSpec((ns, N_HEADS, HEAD, HEAD), lambda b, l: (b, 0, 0, 0))
    scratch = ([pltpu.VMEM((ns, 1, N_SHIFT_PAD), F32)]
               + [pltpu.VMEM((ns * tb, D_RWKV), F32)] * 9)
    return pl.pallas_call(
        functools.partial(_rwkv_kernel, ns=ns, tb=tb),
        grid=(bsz // ns, nl),
        in_specs=[col(0), col(1), col(2), col(COL_LORA, N_LORA_PAD),
                  pl.BlockSpec((ns, 1, N_SHIFT_PAD), lambda b, l: (b, 0, 0)),
                  state_spec] + [_const_spec(c) for c in consts],
        out_specs=[pl.BlockSpec((ns, tb, D_RWKV), lambda b, l: (b, l, 0)), state_spec],
        out_shape=[jax.ShapeDtypeStruct((bsz, seq, D_RWKV), BF16),
                   jax.ShapeDtypeStruct((bsz, N_HEADS, HEAD, HEAD), F32)],
        scratch_shapes=scratch,
        compiler_params=pltpu.CompilerParams(
            dimension_semantics=("parallel", "arbitrary"), vmem_limit_bytes=VMEM_LIMIT),
    )(proj3, proj3, proj3, proj3, shift0p, wkv0, *consts)


def _s5_kernel(u_ref, hre0_ref, him0_ref, lre_ref, lim_ref, wbre_ref, wbim_ref, wcre_ref,
               wcim_ref, d_ref, gluw_ref, glub_ref,
               o_ref, hre_ref, him_ref,
               *scratch, ns, tb):
    xre, xim, hre_s, him_s = (scratch[i * ns:(i + 1) * ns] for i in range(4))

    @pl.when(pl.program_id(1) == 0)
    def _():
        hre_ref[...] = hre0_ref[...]
        him_ref[...] = him0_ref[...]

    kin = S5_BUNDLE * S5_GROUP
    kst = S5_BUNDLE * S5_STATE
    lre = lre_ref[...]
    lim = lim_ref[...]

    def in_map(s):
        u = u_ref[s]
        for m in range(S5_GROUPS // S5_BUNDLE):
            um = u[:, kin * m:kin * (m + 1)].astype(BF16)
            xre[s][:, kst * m:kst * (m + 1)] = _dot(um, wbre_ref[m])
            xim[s][:, kst * m:kst * (m + 1)] = _dot(um, wbim_ref[m])

    def scan(s):
        hre, him = hre_ref[s], him_ref[s]
        for t in range(tb):
            hre, him = (lre * hre - lim * him + xre[s][t:t + 1, :],
                        lre * him + lim * hre + xim[s][t:t + 1, :])
            hre_s[s][t:t + 1, :] = hre
            him_s[s][t:t + 1, :] = him
        hre_ref[s] = hre
        him_ref[s] = him

    def out_map(s):
        ys = []
        for m in range(S5_GROUPS // S5_BUNDLE):
            ys.append(_dot(hre_s[s][:, kst * m:kst * (m + 1)].astype(BF16), wcre_ref[m])
                      + _dot(him_s[s][:, kst * m:kst * (m + 1)].astype(BF16), wcim_ref[m]))
        y = jnp.concatenate(ys, axis=1) + d_ref[...] * u_ref[s]
        zg = 0.5 * y * (1.0 + jnp.tanh(math.sqrt(2.0 / math.pi) * (y + 0.044715 * (y * y * y))))
        gate = jax.nn.sigmoid(_dot(zg.astype(BF16), gluw_ref[...]) + glub_ref[...])
        o_ref[s] = (zg * gate).astype(o_ref.dtype)

    in_map(0)
    for s in range(ns):
        if s + 1 < ns:
            in_map(s + 1)
        scan(s)
        if s > 0:
            out_map(s - 1)
    out_map(ns - 1)


_S5_SEQS = 2


def _s5(proj3, hre0, him0, p):
    bsz, seq, _ = proj3.shape
    ns = _S5_SEQS
    assert bsz % ns == 0
    tb = 128 if seq % 128 == 0 else CHUNK
    nl = seq // tb
    consts = [p["lam_re"], p["lam_im"], p["wb_re"], p["wb_im"], p["wc_re"], p["wc_im"],
              p["s5_d"], p["glu_w"], p["glu_b"]]
    state_spec = pl.BlockSpec((ns, 1, S5_WIDE), lambda b, l: (b, 0, 0))
    return pl.pallas_call(
        functools.partial(_s5_kernel, ns=ns, tb=tb),
        grid=(bsz // ns, nl),
        in_specs=[pl.BlockSpec((ns, tb, D_S5), lambda b, l: (b, l, COL_U)),
                  state_spec, state_spec] + [_const_spec(c) for c in consts],
        out_specs=[pl.BlockSpec((ns, tb, D_S5), lambda b, l: (b, l, 0)), state_spec, state_spec],
        out_shape=[jax.ShapeDtypeStruct((bsz, seq, D_S5), BF16),
                   jax.ShapeDtypeStruct((bsz, 1, S5_WIDE), F32),
                   jax.ShapeDtypeStruct((bsz, 1, S5_WIDE), F32)],
        scratch_shapes=[pltpu.VMEM((tb, S5_WIDE), F32)] * (4 * ns),
        compiler_params=pltpu.CompilerParams(
            dimension_semantics=("parallel", "arbitrary"), vmem_limit_bytes=VMEM_LIMIT),
    )(proj3, hre0, him0, *consts)


def _out_ln_kernel(x_ref, orw_ref, os5_ref, ging_ref, binb_ref, wa_ref, wb_ref, g_ref, b_ref, o_ref):
    xn = _layer_norm(x_ref[...], ging_ref[...], binb_ref[...])
    mix = _dot(orw_ref[...], wa_ref[...]) + _dot(os5_ref[...], wb_ref[...])
    o_ref[...] = _layer_norm(ALPHA * xn + mix, g_ref[...], b_ref[...])


def _out_ln(x2d, o_rw, o_s5, p):
    t, d = x2d.shape
    tm = _row_tile(t, 512)
    consts = [p["ln_in_g"], p["ln_in_b"], p["w_out_a"], p["w_out_b"], p["ln1_g"], p["ln1_b"]]
    return pl.pallas_call(
        _out_ln_kernel,
        grid=(t // tm,),
        in_specs=[pl.BlockSpec((tm, d), lambda i: (i, 0)),
                  pl.BlockSpec((tm, D_RWKV), lambda i: (i, 0)),
                  pl.BlockSpec((tm, D_S5), lambda i: (i, 0))] + [_const_spec(c) for c in consts],
        out_specs=pl.BlockSpec((tm, d), lambda i: (i, 0)),
        out_shape=jax.ShapeDtypeStruct((t, d), F32),
        compiler_params=pltpu.CompilerParams(
            dimension_semantics=("parallel",), vmem_limit_bytes=VMEM_LIMIT),
    )(x2d, o_rw, o_s5, *consts)


def _ffn_ln_kernel(x_ref, w1_ref, w3_ref, w2_ref, g_ref, b_ref, o_ref, xb_ref, *, nf):
    f = pl.program_id(1)

    @pl.when(f == 0)
    def _():
        xb_ref[...] = x_ref[...].astype(BF16)
        o_ref[...] = jnp.zeros_like(o_ref)

    xb = xb_ref[...]
    h1 = _dot(xb, w1_ref[...])
    h3 = _dot(xb, w3_ref[...])
    act = (h1 * jax.nn.sigmoid(h1) * h3).astype(BF16)
    o_ref[...] += _dot(act, w2_ref[...])

    @pl.when(f == nf - 1)
    def _():
        o_ref[...] = _layer_norm(ALPHA * x_ref[...] + o_ref[...], g_ref[...], b_ref[...])


def _ffn_ln(x1, p):
    t, d = x1.shape
    dff = p["ffn_w1"].shape[1]
    tm = _row_tile(t, 1024)
    tf = 512
    nf = dff // tf
    return pl.pallas_call(
        functools.partial(_ffn_ln_kernel, nf=nf),
        grid=(t // tm, nf),
        in_specs=[pl.BlockSpec((tm, d), lambda i, f: (i, 0), pipeline_mode=pl.Buffered(1)),
                  pl.BlockSpec((d, tf), lambda i, f: (0, f)),
                  pl.BlockSpec((d, tf), lambda i, f: (0, f)),
                  pl.BlockSpec((tf, d), lambda i, f: (f, 0)),
                  pl.BlockSpec((1, d), lambda i, f: (0, 0)),
                  pl.BlockSpec((1, d), lambda i, f: (0, 0))],
        out_specs=pl.BlockSpec((tm, d), lambda i, f: (i, 0)),
        out_shape=jax.ShapeDtypeStruct((t, d), F32),
        scratch_shapes=[pltpu.VMEM((tm, d), BF16)],
        compiler_params=pltpu.CompilerParams(
            dimension_semantics=("parallel", "arbitrary"), vmem_limit_bytes=VMEM_LIMIT),
    )(x1, p["ffn_w1"], p["ffn_w3"], p["ffn_w2"], p["ln2_g"], p["ln2_b"])


def _pad_cols(x, n):
    return jnp.pad(x, [(0, 0)] * (x.ndim - 1) + [(0, n - x.shape[-1])])


def _shift_layout(x):
    return jnp.concatenate([x[..., :3 * D_RWKV], _pad_cols(x[..., 3 * D_RWKV:], N_LORA_PAD)], axis=-1)


def _block_diag(blocks):
    nb, g, a, b = blocks.shape
    eye = jnp.eye(g, dtype=blocks.dtype)
    return jnp.einsum("mgab,gh->mgahb", blocks, eye).reshape(nb, g * a, g * b)


def _prepare(ln_in_g, ln_in_b, w_in, mu_shift, w0, w_lora_up, a0, a_lora_up, g_lora_up, k_k, k_a,
             r_k, gn_g, gn_b, s5_a_re, s5_a_im, s5_log_dt, s5_b_re, s5_b_im, s5_c_re, s5_c_im,
             s5_d, glu_w, glu_b, w_out, ln1_g, ln1_b, ffn_w1, ffn_w3, ffn_w2, ln2_g, ln2_b):
    row = lambda x: x.reshape(1, -1).astype(F32)
    w = w_in[0]
    o3 = 3 * D_RWKV
    w_in_p = jnp.concatenate(
        [w[:, :o3], w[:, N_SHIFT:], _pad_cols(w[:, o3:N_SHIFT], N_LORA_PAD)], axis=1).astype(BF16)
    wup = jnp.zeros((128, D_RWKV), F32).at[:LORA_W].set(w_lora_up[0]).astype(BF16)
    aup = jnp.zeros((128, D_RWKV), F32).at[LORA_W:LORA_W + LORA_A].set(a_lora_up[0]).astype(BF16)
    gup = jnp.zeros((256, D_RWKV), F32).at[:LORA_G].set(g_lora_up[0]).astype(BF16)
    idx = jnp.arange(256)
    tri = ((idx[:, None] >= idx[None, :]) & (idx[:, None] // CHUNK == idx[None, :] // CHUNK)).astype(BF16)
    ones = (idx[:, None] // HEAD == idx[None, :] // HEAD).astype(BF16)

    dt = jnp.exp(s5_log_dt[0].astype(F32))[:, None]
    a_re, a_im = s5_a_re[0].astype(F32), s5_a_im[0].astype(F32)
    mag = jnp.exp(a_re * dt)
    lam_re, lam_im = mag * jnp.cos(a_im * dt), mag * jnp.sin(a_im * dt)
    den = a_re * a_re + a_im * a_im
    q_re = ((lam_re - 1.0) * a_re + lam_im * a_im) / den
    q_im = (lam_im * a_re - (lam_re - 1.0) * a_im) / den
    b_re, b_im = s5_b_re[0].astype(F32), s5_b_im[0].astype(F32)
    bb_re = q_re[..., None] * b_re - q_im[..., None] * b_im
    bb_im = q_re[..., None] * b_im + q_im[..., None] * b_re
    nb = S5_GROUPS // S5_BUNDLE
    to_in = lambda x: _block_diag(jnp.swapaxes(x, 1, 2).reshape(nb, S5_BUNDLE, S5_GROUP, S5_STATE))
    to_out = lambda x: _block_diag(jnp.swapaxes(x, 1, 2).reshape(nb, S5_BUNDLE, S5_STATE, S5_GROUP))
    return {
        "ln_in_g": row(ln_in_g), "ln_in_b": row(ln_in_b), "w_in_p": w_in_p,
        "mu_p": _shift_layout(row(mu_shift[0])), "w0": row(w0[0]), "wup": wup, "a0": row(a0[0]),
        "aup": aup, "gup": gup, "k_k": row(k_k[0]), "k_a": row(k_a[0]), "r_k": row(r_k[0]),
        "gn_g": row(gn_g[0]), "gn_b": row(gn_b[0]), "tri": tri, "ones": ones,
        "lam_re": lam_re.reshape(1, -1), "lam_im": lam_im.reshape(1, -1),
        "wb_re": to_in(bb_re).astype(BF16), "wb_im": to_in(bb_im).astype(BF16),
        "wc_re": to_out(s5_c_re[0].astype(F32)).astype(BF16),
        "wc_im": to_out(-s5_c_im[0].astype(F32)).astype(BF16),
        "s5_d": row(s5_d[0]), "glu_w": glu_w[0].astype(BF16), "glu_b": row(glu_b[0]),
        "w_out_a": w_out[0, :D_RWKV].astype(BF16), "w_out_b": w_out[0, D_RWKV:].astype(BF16),
        "ln1_g": row(ln1_g[0]), "ln1_b": row(ln1_b[0]),
        "ffn_w1": ffn_w1[0].astype(BF16), "ffn_w3": ffn_w3[0].astype(BF16),
        "ffn_w2": ffn_w2[0].astype(BF16), "ln2_g": row(ln2_g[0]), "ln2_b": row(ln2_b[0]),
    }


def _run(x, shift, wkv, h_re, h_im, p):
    bsz, seq, d = x.shape
    x2d = x.reshape(bsz * seq, d)
    proj = _ln_proj(x2d, p["ln_in_g"], p["ln_in_b"], p["w_in_p"])
    proj3 = proj.reshape(bsz, seq, N_PROJ_PAD)
    o_rw, n_wkv = _rwkv(proj3, _shift_layout(shift[0]), wkv[0], p)
    o_s5, n_hre, n_him = _s5(proj3, h_re[0].reshape(bsz, 1, S5_WIDE),
                             h_im[0].reshape(bsz, 1, S5_WIDE), p)
    x1 = _out_ln(x2d, o_rw.reshape(bsz * seq, D_RWKV), o_s5.reshape(bsz * seq, D_S5), p)
    y = _ffn_ln(x1, p).reshape(bsz, seq, d)
    last = proj3[:, seq - 1:, :]
    lo0 = COL_LORA * N_LORA_PAD
    n_shift = jnp.concatenate([last[..., :3 * D_RWKV], last[..., lo0:lo0 + N_LORA]], axis=-1)
    return (y, n_shift[None], n_wkv[None],
            n_hre.reshape(1, bsz, S5_GROUPS, S5_STATE), n_him.reshape(1, bsz, S5_GROUPS, S5_STATE))


def kernel(x_prompt, x_sample, cache_shift, state_wkv, state_s5_re, state_s5_im, ln_in_g, ln_in_b, w_in, mu_shift, w0, w_lora_up, a0, a_lora_up, g_lora_up, k_k, k_a, r_k, gn_g, gn_b, s5_a_re, s5_a_im, s5_log_dt, s5_b_re, s5_b_im, s5_c_re, s5_c_im, s5_d, glu_w, glu_b, w_out, ln1_g, ln1_b, ffn_w1, ffn_w3, ffn_w2, ln2_g, ln2_b):
    p = _prepare(ln_in_g, ln_in_b, w_in, mu_shift, w0, w_lora_up, a0, a_lora_up, g_lora_up, k_k,
                 k_a, r_k, gn_g, gn_b, s5_a_re, s5_a_im, s5_log_dt, s5_b_re, s5_b_im, s5_c_re,
                 s5_c_im, s5_d, glu_w, glu_b, w_out, ln1_g, ln1_b, ffn_w1, ffn_w3, ffn_w2, ln2_g,
                 ln2_b)
    bp = x_prompt.shape[0]
    dt = x_prompt.dtype
    zeros = lambda *s: jnp.zeros((1, bp) + s, dt)
    yp, sh_p, wkv_p, re_p, im_p = _run(
        x_prompt, zeros(1, N_SHIFT), zeros(N_HEADS, HEAD, HEAD), zeros(S5_GROUPS, S5_STATE),
        zeros(S5_GROUPS, S5_STATE), p)
    ys, sh_s, wkv_s, re_s, im_s = _run(x_sample, cache_shift, state_wkv, state_s5_re, state_s5_im, p)
    return (yp, ys, sh_p, wkv_p, re_p, im_p, sh_s, wkv_s, re_s, im_s)
```

```python
import functools
import math

import jax
import jax.numpy as jnp
from jax import lax
from jax.experimental import pallas as pl
from jax.experimental.pallas import tpu as pltpu

F32 = jnp.float32
BF16 = jnp.bfloat16

D_MODEL = 2048
D_RWKV = 1024
D_S5 = 1024
HEAD = 64
N_HEADS = D_RWKV // HEAD
LORA_W = 64
LORA_A = 64
LORA_G = 160
N_LORA = LORA_W + LORA_A + LORA_G
N_LORA_PAD = 512
N_SHIFT = 3 * D_RWKV + N_LORA
N_SHIFT_PAD = 3 * D_RWKV + N_LORA_PAD
N_PROJ_PAD = 3 * D_RWKV + D_S5 + N_LORA_PAD
COL_U = 3
COL_LORA = 8
S5_GROUP = 16
S5_GROUPS = D_S5 // S5_GROUP
S5_STATE = 64
S5_WIDE = S5_GROUPS * S5_STATE
S5_BUNDLE = 8
CHUNK = 64
LN_EPS = 1e-5
GN_EPS = 64e-5
ALPHA = 2.0 ** 0.25
VMEM_LIMIT = 56 * 1024 * 1024
_LN_ROWS = 256


def _split(x):
    hi = x.astype(BF16)
    lo = (x - hi.astype(F32)).astype(BF16)
    return hi, lo


def _dot(a, b, dims=((1,), (0,))):
    return lax.dot_general(a, b, (dims, ((), ())), preferred_element_type=F32)


def _mm(a, b, dims=((1,), (0,)), passes=1):
    if passes == 1:
        return _dot(a.astype(BF16), b.astype(BF16), dims)
    ah, al = _split(a)
    if passes == 2:
        bh = b.astype(BF16)
        return _dot(ah, bh, dims) + _dot(al, bh, dims)
    bh, bl = _split(b)
    return _dot(ah, bh, dims) + (_dot(al, bh, dims) + _dot(ah, bl, dims))


def _layer_norm(x, g, b):
    mu = jnp.mean(x, axis=-1, keepdims=True)
    xc = x - mu
    var = jnp.mean(xc * xc, axis=-1, keepdims=True)
    return xc * lax.rsqrt(var + LN_EPS) * g + b


def _row_tile(t, cap):
    for c in (1024, 768, 512, 256, 128, 64):
        if c <= cap and t % c == 0:
            return c
    raise ValueError(f"token count {t} is not a multiple of 64")


def _const_spec(arr):
    nd = arr.ndim
    return pl.BlockSpec(arr.shape, lambda *_: (0,) * nd)


def _ln_proj_kernel(x_ref, g_ref, b_ref, w_ref, o_ref):
    for r in range(0, x_ref.shape[0], _LN_ROWS):
        rows = slice(r, r + _LN_ROWS)
        xn = _layer_norm(x_ref[rows, :], g_ref[...], b_ref[...]).astype(BF16)
        o_ref[rows, :] = _dot(xn, w_ref[...])


def _ln_proj(x2d, g, b, w_bf16):
    t, d = x2d.shape
    n = w_bf16.shape[1]
    tm = _row_tile(t, 1024)
    tn = n // 3
    return pl.pallas_call(
        _ln_proj_kernel,
        grid=(t // tm, n // tn),
        in_specs=[
            pl.BlockSpec((tm, d), lambda i, j: (i, 0)),
            pl.BlockSpec((1, d), lambda i, j: (0, 0)),
            pl.BlockSpec((1, d), lambda i, j: (0, 0)),
            pl.BlockSpec((d, tn), lambda i, j: (0, j)),
        ],
        out_specs=pl.BlockSpec((tm, tn), lambda i, j: (i, j)),
        out_shape=jax.ShapeDtypeStruct((t, n), F32),
        compiler_params=pltpu.CompilerParams(
            dimension_semantics=("parallel", "arbitrary"), vmem_limit_bytes=VMEM_LIMIT),
    )(x2d, g, b, w_bf16)


def _segsum(x, ones):
    xb = x.astype(BF16)
    return jnp.concatenate(
        [_dot(xb[:, 256 * q:256 * (q + 1)], ones) for q in range(x.shape[1] // 256)], axis=1)


def _rwkv_kernel(pr_ref, pk_ref, pv_ref, plo_ref, sh_ref, s0_ref, mu_ref, w0_ref, wup_ref,
                 a0_ref, aup_ref, gup_ref, kkw_ref, kaw_ref, rkw_ref, gng_ref, gnb_ref,
                 tri_ref, ones_ref,
                 o_ref, st_ref,
                 carry_ref, r_s, k_s, v_s, kk_s, b_s, ld_s, lc_s, g_s, y_s,
                 *, ns, tb):
    @pl.when(pl.program_id(1) == 0)
    def _():
        carry_ref[...] = sh_ref[...]
        st_ref[...] = s0_ref[...]

    row = lax.broadcasted_iota(jnp.int32, (tb, 1), 0)

    def token_shift(x_ref, c0):
        width = x_ref.shape[-1]
        outs = []
        for s in range(ns):
            x = x_ref[s]
            prev = jnp.where(row == 0, carry_ref[s, :, c0:c0 + width], pltpu.roll(x, 1, 0))
            outs.append(x + (prev - x) * mu_ref[:, c0:c0 + width])
            carry_ref[s, :, c0:c0 + width] = x[tb - 1:tb, :]
        return jnp.concatenate(outs, axis=0)

    ones = ones_ref[...]
    r = token_shift(pr_ref, 0)
    k = token_shift(pk_ref, D_RWKV)
    v = token_shift(pv_ref, 2 * D_RWKV)
    lo = token_shift(plo_ref, 3 * D_RWKV)

    lane = lax.broadcasted_iota(jnp.int32, (ns * tb, 128), 1)
    lo_wa = lo[:, 0:128]
    act_wa = jnp.where(lane < LORA_W, jnp.tanh(lo_wa), lo_wa).astype(BF16)
    act_g = jax.nn.sigmoid(lo[:, 128:384]).astype(BF16)
    w = w0_ref[...] + _dot(act_wa, wup_ref[...])
    ld = (-math.exp(-0.5)) * jax.nn.sigmoid(w)
    a = jax.nn.sigmoid(a0_ref[...] + _dot(act_wa, aup_ref[...]))
    g_s[...] = _dot(act_g, gup_ref[...])

    kk = k * kkw_ref[...]
    kk = kk * jnp.minimum(lax.rsqrt(_segsum(kk * kk, ones)), 1e12)
    k = k * (1.0 + (a - 1.0) * kaw_ref[...])

    tri = tri_ref[...]
    h1 = ld.astype(BF16)
    r1 = ld - h1.astype(F32)
    h2 = r1.astype(BF16)
    h3 = (r1 - h2.astype(F32)).astype(BF16)
    for s in range(ns):
        rows = slice(s * tb, (s + 1) * tb)
        lc_s[rows, :] = _dot(tri, h1[rows]) + (_dot(tri, h2[rows]) + _dot(tri, h3[rows]))
    ld_s[...] = ld
    r_s[...] = r
    k_s[...] = k
    v_s[...] = v
    kk_s[...] = kk
    b_s[...] = kk * a

    rr = lax.broadcasted_iota(jnp.int32, (2 * CHUNK, 2 * CHUNK), 0)
    cc = lax.broadcasted_iota(jnp.int32, (2 * CHUNK, 2 * CHUNK), 1)
    causal = cc % CHUNK < rr % CHUNK + jnp.where(rr < CHUNK, 1, 0)
    t_i = lax.broadcasted_iota(jnp.int32, (CHUNK, CHUNK), 0)
    s_i = lax.broadcasted_iota(jnp.int32, (CHUNK, CHUNK), 1)

    def same_block(size):
        return t_i // size == s_i // size

    diag_mask = jnp.where(same_block(_SOLVE_BASE), 1.0, 0.0).astype(F32)
    merge_masks = []
    size = _SOLVE_BASE
    while size < CHUNK:
        merge_masks.append(jnp.where(
            same_block(2 * size) & jnp.logical_not(same_block(size)), 1.0, 0.0).astype(F32))
        size *= 2
    lane_z = lax.broadcasted_iota(jnp.int32, (CHUNK, 4 * HEAD), 1)
    eye = jnp.where(t_i == s_i, 1.0, 0.0).astype(F32)

    def chunk_body(c, _):
        c0 = pl.multiple_of(c * CHUNK, CHUNK)
        units = []
        per_seq = []
        for s in range(ns):
            sl = pl.ds(s * tb + c0, CHUNK)
            lc = lc_s[sl, :]
            e_inc = jnp.exp(lc)
            e_exc = jnp.exp(lc - ld_s[sl, :])
            e_inv = jnp.exp(-lc)
            lc_end = lc[CHUNK - 1:CHUNK, :]
            e_hat = jnp.exp(lc_end - lc)
            kc = k_s[sl, :]
            bc = b_s[sl, :]
            per_seq.append(dict(
                sl=sl, e_end=jnp.exp(lc_end), vc=v_s[sl, :], rt=r_s[sl, :] * e_inc,
                kap=kk_s[sl, :] * e_exc, kt=kc * e_inv, bt=bc * e_inv, khat=kc * e_hat,
                bhat=bc * e_hat))
            units += [(s, h, slice(HEAD * h, HEAD * (h + 1))) for h in range(N_HEADS)]

        gms = []
        for s, h, hs in units:
            q = per_seq[s]
            lhs = jnp.concatenate([q["rt"][:, hs], q["kap"][:, hs]], axis=0)
            rhs = jnp.concatenate([q["kt"][:, hs], q["bt"][:, hs]], axis=0)
            gms.append(jnp.where(causal, _mm(lhs, rhs, ((1,), (1,))), 0.0))
        nmats = [-gm[CHUNK:, HEAD:2 * HEAD] for gm in gms]
        zeds = []
        for (s, h, hs), gm, nm in zip(units, gms, nmats):
            q = per_seq[s]
            akk_v = _mm(gm[CHUNK:, 0:HEAD], q["vc"][:, hs])
            zeds.append(jnp.concatenate(
                [nm * diag_mask, eye, akk_v, q["kap"][:, hs]], axis=1))
        steps = _SOLVE_BASE.bit_length() - 1
        for it in range(steps):
            ress = [_mm(zed[:, 0:HEAD], zed) for zed in zeds]
            if it < steps - 1:
                zeds = [jnp.where(lane_z < HEAD, res, zed + res) for zed, res in zip(zeds, ress)]
            else:
                zeds = [zed + res for zed, res in zip(zeds, ress)]
        for mask in merge_masks:
            ress = [_mm(nm * mask, zed) for nm, zed in zip(nmats, zeds)]
            ress = [_mm(zed[:, HEAD:2 * HEAD], res) for zed, res in zip(zeds, ress)]
            zeds = [zed + res for zed, res in zip(zeds, ress)]
        s_hs = [st_ref[s, h] for s, h, _ in units]
        ds = [_mm(jnp.concatenate([per_seq[s]["rt"][:, hs], zed[:, 3 * HEAD:4 * HEAD]], axis=0),
                  s_h, ((1,), (1,)))
              for (s, h, hs), zed, s_h in zip(units, zeds, s_hs)]
        vus = [jnp.concatenate([per_seq[s]["vc"][:, hs],
                                -(zed[:, 2 * HEAD:3 * HEAD] + d[CHUNK:, :])], axis=0)
               for (s, h, hs), zed, d in zip(units, zeds, ds)]
        for (s, h, hs), gm, d, vu in zip(units, gms, ds, vus):
            y_s[per_seq[s]["sl"], hs] = d[:CHUNK, :] + _mm(gm[:CHUNK, :], vu)
        for (s, h, hs), s_h, vu in zip(units, s_hs, vus):
            q = per_seq[s]
            kb_hat = jnp.concatenate([q["khat"][:, hs], q["bhat"][:, hs]], axis=0)
            st_ref[s, h] = s_h * q["e_end"][:, hs] + _mm(vu, kb_hat, ((0,), (0,)))
        return 0

    lax.fori_loop(0, tb // CHUNK, chunk_body, 0)

    y = y_s[...]
    mu = _segsum(y, ones) * (1.0 / HEAD)
    yc = y - mu
    var = _segsum(yc * yc, ones) * (1.0 / HEAD)
    yn = yc * lax.rsqrt(var + GN_EPS) * gng_ref[...] + gnb_ref[...]
    bonus = _segsum(r_s[...] * k_s[...] * rkw_ref[...], ones) * v_s[...]
    o_ref[...] = ((yn + bonus) * g_s[...]).astype(o_ref.dtype).reshape(ns, tb, D_RWKV)


_SOLVE_BASE = 8
_RWKV_SEQS = 2


def _rwkv(proj3, shift0p, wkv0, p):
    bsz, seq, _ = proj3.shape
    ns = _RWKV_SEQS
    assert bsz % ns == 0
    tb = 256 if seq % 256 == 0 else CHUNK
    nl = seq // tb

    def col(cb, width=D_RWKV):
        return pl.BlockSpec((ns, tb, width), lambda b, l, cb=cb: (b, l, cb))

    consts = [p["mu_p"], p["w0"], p["wup"], p["a0"], p["aup"], p["gup"], p["k_k"], p["k_a"],
              p["r_k"], p["gn_g"], p["gn_b"], p["tri"][:tb, :tb], p["ones"]]
    state_spec = pl.BlockSpec((ns, N_HEADS, HEAD, HEAD), lambda b, l: (b, 0, 0, 0))
    scratch = ([pltpu.VMEM((ns, 1, N_SHIFT_PAD), F32)]
               + [pltpu.VMEM((ns * tb, D_RWKV), F32)] * 9)
    return pl.pallas_call(
        functools.partial(_rwkv_kernel, ns=ns, tb=tb),
        grid=(bsz // ns, nl),
        in_specs=[col(0), col(1), col(2), col(COL_LORA, N_LORA_PAD),
                  pl.BlockSpec((ns, 1, N_SHIFT_PAD), lambda b, l: (b, 0, 0)),
                  state_spec] + [_const_spec(c) for c in consts],
        out_specs=[pl.BlockSpec((ns, tb, D_RWKV), lambda b, l: (b, l, 0)), state_spec],
        out_shape=[jax.ShapeDtypeStruct((bsz, seq, D_RWKV), BF16),
                   jax.ShapeDtypeStruct((bsz, N_HEADS, HEAD, HEAD), F32)],
        scratch_shapes=scratch,
        compiler_params=pltpu.CompilerParams(
            dimension_semantics=("parallel", "arbitrary"), vmem_limit_bytes=VMEM_LIMIT),
    )(proj3, proj3, proj3, proj3, shift0p, wkv0, *consts)


def _s5_kernel(u_ref, hre0_ref, him0_ref, lre_ref, lim_ref, wbre_ref, wbim_ref, wcre_ref,
               wcim_ref, d_ref, gluw_ref, glub_ref,
               o_ref, hre_ref, him_ref,
               *scratch, ns, tb):
    xre, xim, hre_s, him_s = (scratch[i * ns:(i + 1) * ns] for i in range(4))

    @pl.when(pl.program_id(1) == 0)
    def _():
        hre_ref[...] = hre0_ref[...]
        him_ref[...] = him0_ref[...]

    kin = S5_BUNDLE * S5_GROUP
    kst = S5_BUNDLE * S5_STATE
    lre = lre_ref[...]
    lim = lim_ref[...]

    def in_map(s):
        u = u_ref[s]
        for m in range(S5_GROUPS // S5_BUNDLE):
            um = u[:, kin * m:kin * (m + 1)].astype(BF16)
            xre[s][:, kst * m:kst * (m + 1)] = _dot(um, wbre_ref[m])
            xim[s][:, kst * m:kst * (m + 1)] = _dot(um, wbim_ref[m])

    def scan(s):
        hre, him = hre_ref[s], him_ref[s]
        for t in range(tb):
            hre, him = (lre * hre - lim * him + xre[s][t:t + 1, :],
                        lre * him + lim * hre + xim[s][t:t + 1, :])
            hre_s[s][t:t + 1, :] = hre
            him_s[s][t:t + 1, :] = him
        hre_ref[s] = hre
        him_ref[s] = him

    def out_map(s):
        ys = []
        for m in range(S5_GROUPS // S5_BUNDLE):
            ys.append(_dot(hre_s[s][:, kst * m:kst * (m + 1)].astype(BF16), wcre_ref[m])
                      + _dot(him_s[s][:, kst * m:kst * (m + 1)].astype(BF16), wcim_ref[m]))
        y = jnp.concatenate(ys, axis=1) + d_ref[...] * u_ref[s]
        zg = 0.5 * y * (1.0 + jnp.tanh(math.sqrt(2.0 / math.pi) * (y + 0.044715 * (y * y * y))))
        gate = jax.nn.sigmoid(_dot(zg.astype(BF16), gluw_ref[...]) + glub_ref[...])
        o_ref[s] = (zg * gate).astype(o_ref.dtype)

    in_map(0)
    for s in range(ns):
        if s + 1 < ns:
            in_map(s + 1)
        scan(s)
        if s > 0:
            out_map(s - 1)
    out_map(ns - 1)


_S5_SEQS = 2


def _s5(proj3, hre0, him0, p):
    bsz, seq, _ = proj3.shape
    ns = _S5_SEQS
    assert bsz % ns == 0
    tb = 128 if seq % 128 == 0 else CHUNK
    nl = seq // tb
    consts = [p["lam_re"], p["lam_im"], p["wb_re"], p["wb_im"], p["wc_re"], p["wc_im"],
              p["s5_d"], p["glu_w"], p["glu_b"]]
    state_spec = pl.BlockSpec((ns, 1, S5_WIDE), lambda b, l: (b, 0, 0))
    return pl.pallas_call(
        functools.partial(_s5_kernel, ns=ns, tb=tb),
        grid=(bsz // ns, nl),
        in_specs=[pl.BlockSpec((ns, tb, D_S5), lambda b, l: (b, l, COL_U)),
                  state_spec, state_spec] + [_const_spec(c) for c in consts],
        out_specs=[pl.BlockSpec((ns, tb, D_S5), lambda b, l: (b, l, 0)), state_spec, state_spec],
        out_shape=[jax.ShapeDtypeStruct((bsz, seq, D_S5), BF16),
                   jax.ShapeDtypeStruct((bsz, 1, S5_WIDE), F32),
                   jax.ShapeDtypeStruct((bsz, 1, S5_WIDE), F32)],
        scratch_shapes=[pltpu.VMEM((tb, S5_WIDE), F32)] * (4 * ns),
        compiler_params=pltpu.CompilerParams(
            dimension_semantics=("parallel", "arbitrary"), vmem_limit_bytes=VMEM_LIMIT),
    )(proj3, hre0, him0, *consts)


def _out_ln_kernel(x_ref, orw_ref, os5_ref, ging_ref, binb_ref, wa_ref, wb_ref, g_ref, b_ref, o_ref):
    for r in range(0, x_ref.shape[0], _LN_ROWS):
        rows = slice(r, r + _LN_ROWS)
        xn = _layer_norm(x_ref[rows, :], ging_ref[...], binb_ref[...])
        mix = _dot(orw_ref[rows, :], wa_ref[...]) + _dot(os5_ref[rows, :], wb_ref[...])
        o_ref[rows, :] = _layer_norm(ALPHA * xn + mix, g_ref[...], b_ref[...])


def _out_ln(x2d, o_rw, o_s5, p):
    t, d = x2d.shape
    tm = _row_tile(t, 512)
    consts = [p["ln_in_g"], p["ln_in_b"], p["w_out_a"], p["w_out_b"], p["ln1_g"], p["ln1_b"]]
    return pl.pallas_call(
        _out_ln_kernel,
        grid=(t // tm,),
        in_specs=[pl.BlockSpec((tm, d), lambda i: (i, 0)),
                  pl.BlockSpec((tm, D_RWKV), lambda i: (i, 0)),
                  pl.BlockSpec((tm, D_S5), lambda i: (i, 0))] + [_const_spec(c) for c in consts],
        out_specs=pl.BlockSpec((tm, d), lambda i: (i, 0)),
        out_shape=jax.ShapeDtypeStruct((t, d), F32),
        compiler_params=pltpu.CompilerParams(
            dimension_semantics=("parallel",), vmem_limit_bytes=VMEM_LIMIT),
    )(x2d, o_rw, o_s5, *consts)


def _ffn_ln_kernel(x_ref, w1_ref, w3_ref, w2_ref, g_ref, b_ref, o_ref, xb_ref, *, nf):
    f = pl.program_id(1)

    @pl.when(f == 0)
    def _():
        xb_ref[...] = x_ref[...].astype(BF16)
        o_ref[...] = jnp.zeros_like(o_ref)

    xb = xb_ref[...]
    h1 = _dot(xb, w1_ref[...])
    h3 = _dot(xb, w3_ref[...])
    act = (h1 * jax.nn.sigmoid(h1) * h3).astype(BF16)
    o_ref[...] += _dot(act, w2_ref[...])

    @pl.when(f == nf - 1)
    def _():
        o_ref[...] = _layer_norm(ALPHA * x_ref[...] + o_ref[...], g_ref[...], b_ref[...])


def _ffn_ln(x1, p):
    t, d = x1.shape
    dff = p["ffn_w1"].shape[1]
    tm = _row_tile(t, 512)
    tf = 512
    nf = dff // tf
    return pl.pallas_call(
        functools.partial(_ffn_ln_kernel, nf=nf),
        grid=(t // tm, nf),
        in_specs=[pl.BlockSpec((tm, d), lambda i, f: (i, 0)),
                  pl.BlockSpec((d, tf), lambda i, f: (0, f)),
                  pl.BlockSpec((d, tf), lambda i, f: (0, f)),
                  pl.BlockSpec((tf, d), lambda i, f: (f, 0)),
                  pl.BlockSpec((1, d), lambda i, f: (0, 0)),
                  pl.BlockSpec((1, d), lambda i, f: (0, 0))],
        out_specs=pl.BlockSpec((tm, d), lambda i, f: (i, 0)),
        out_shape=jax.ShapeDtypeStruct((t, d), F32),
        scratch_shapes=[pltpu.VMEM((tm, d), BF16)],
        compiler_params=pltpu.CompilerParams(
            dimension_semantics=("parallel", "arbitrary"), vmem_limit_bytes=VMEM_LIMIT),
    )(x1, p["ffn_w1"], p["ffn_w3"], p["ffn_w2"], p["ln2_g"], p["ln2_b"])


def _pad_cols(x, n):
    return jnp.pad(x, [(0, 0)] * (x.ndim - 1) + [(0, n - x.shape[-1])])


def _shift_layout(x):
    return jnp.concatenate([x[..., :3 * D_RWKV], _pad_cols(x[..., 3 * D_RWKV:], N_LORA_PAD)], axis=-1)


def _block_diag(blocks):
    nb, g, a, b = blocks.shape
    eye = jnp.eye(g, dtype=blocks.dtype)
    return jnp.einsum("mgab,gh->mgahb", blocks, eye).reshape(nb, g * a, g * b)


def _prepare(ln_in_g, ln_in_b, w_in, mu_shift, w0, w_lora_up, a0, a_lora_up, g_lora_up, k_k, k_a,
             r_k, gn_g, gn_b, s5_a_re, s5_a_im, s5_log_dt, s5_b_re, s5_b_im, s5_c_re, s5_c_im,
             s5_d, glu_w, glu_b, w_out, ln1_g, ln1_b, ffn_w1, ffn_w3, ffn_w2, ln2_g, ln2_b):
    row = lambda x: x.reshape(1, -1).astype(F32)
    w = w_in[0]
    o3 = 3 * D_RWKV
    w_in_p = jnp.concatenate(
        [w[:, :o3], w[:, N_SHIFT:], _pad_cols(w[:, o3:N_SHIFT], N_LORA_PAD)], axis=1).astype(BF16)
    wup = jnp.zeros((128, D_RWKV), F32).at[:LORA_W].set(w_lora_up[0]).astype(BF16)
    aup = jnp.zeros((128, D_RWKV), F32).at[LORA_W:LORA_W + LORA_A].set(a_lora_up[0]).astype(BF16)
    gup = jnp.zeros((256, D_RWKV), F32).at[:LORA_G].set(g_lora_up[0]).astype(BF16)
    idx = jnp.arange(256)
    tri = ((idx[:, None] >= idx[None, :]) & (idx[:, None] // CHUNK == idx[None, :] // CHUNK)).astype(BF16)
    ones = (idx[:, None] // HEAD == idx[None, :] // HEAD).astype(BF16)

    dt = jnp.exp(s5_log_dt[0].astype(F32))[:, None]
    a_re, a_im = s5_a_re[0].astype(F32), s5_a_im[0].astype(F32)
    mag = jnp.exp(a_re * dt)
    lam_re, lam_im = mag * jnp.cos(a_im * dt), mag * jnp.sin(a_im * dt)
    den = a_re * a_re + a_im * a_im
    q_re = ((lam_re - 1.0) * a_re + lam_im * a_im) / den
    q_im = (lam_im * a_re - (lam_re - 1.0) * a_im) / den
    b_re, b_im = s5_b_re[0].astype(F32), s5_b_im[0].astype(F32)
    bb_re = q_re[..., None] * b_re - q_im[..., None] * b_im
    bb_im = q_re[..., None] * b_im + q_im[..., None] * b_re
    nb = S5_GROUPS // S5_BUNDLE
    to_in = lambda x: _block_diag(jnp.swapaxes(x, 1, 2).reshape(nb, S5_BUNDLE, S5_GROUP, S5_STATE))
    to_out = lambda x: _block_diag(jnp.swapaxes(x, 1, 2).reshape(nb, S5_BUNDLE, S5_STATE, S5_GROUP))
    return {
        "ln_in_g": row(ln_in_g), "ln_in_b": row(ln_in_b), "w_in_p": w_in_p,
        "mu_p": _shift_layout(row(mu_shift[0])), "w0": row(w0[0]), "wup": wup, "a0": row(a0[0]),
        "aup": aup, "gup": gup, "k_k": row(k_k[0]), "k_a": row(k_a[0]), "r_k": row(r_k[0]),
        "gn_g": row(gn_g[0]), "gn_b": row(gn_b[0]), "tri": tri, "ones": ones,
        "lam_re": lam_re.reshape(1, -1), "lam_im": lam_im.reshape(1, -1),
        "wb_re": to_in(bb_re).astype(BF16), "wb_im": to_in(bb_im).astype(BF16),
        "wc_re": to_out(s5_c_re[0].astype(F32)).astype(BF16),
        "wc_im": to_out(-s5_c_im[0].astype(F32)).astype(BF16),
        "s5_d": row(s5_d[0]), "glu_w": glu_w[0].astype(BF16), "glu_b": row(glu_b[0]),
        "w_out_a": w_out[0, :D_RWKV].astype(BF16), "w_out_b": w_out[0, D_RWKV:].astype(BF16),
        "ln1_g": row(ln1_g[0]), "ln1_b": row(ln1_b[0]),
        "ffn_w1": ffn_w1[0].astype(BF16), "ffn_w3": ffn_w3[0].astype(BF16),
        "ffn_w2": ffn_w2[0].astype(BF16), "ln2_g": row(ln2_g[0]), "ln2_b": row(ln2_b[0]),
    }


def _run(x, shift, wkv, h_re, h_im, p):
    bsz, seq, d = x.shape
    x2d = x.reshape(bsz * seq, d)
    proj = _ln_proj(x2d, p["ln_in_g"], p["ln_in_b"], p["w_in_p"])
    proj3 = proj.reshape(bsz, seq, N_PROJ_PAD)
    o_rw, n_wkv = _rwkv(proj3, _shift_layout(shift[0]), wkv[0], p)
    o_s5, n_hre, n_him = _s5(proj3, h_re[0].reshape(bsz, 1, S5_WIDE),
                             h_im[0].reshape(bsz, 1, S5_WIDE), p)
    x1 = _out_ln(x2d, o_rw.reshape(bsz * seq, D_RWKV), o_s5.reshape(bsz * seq, D_S5), p)
    y = _ffn_ln(x1, p).reshape(bsz, seq, d)
    last = proj3[:, seq - 1:, :]
    lo0 = COL_LORA * N_LORA_PAD
    n_shift = jnp.concatenate([last[..., :3 * D_RWKV], last[..., lo0:lo0 + N_LORA]], axis=-1)
    return (y, n_shift[None], n_wkv[None],
            n_hre.reshape(1, bsz, S5_GROUPS, S5_STATE), n_him.reshape(1, bsz, S5_GROUPS, S5_STATE))


def kernel(x_prompt, x_sample, cache_shift, state_wkv, state_s5_re, state_s5_im, ln_in_g, ln_in_b, w_in, mu_shift, w0, w_lora_up, a0, a_lora_up, g_lora_up, k_k, k_a, r_k, gn_g, gn_b, s5_a_re, s5_a_im, s5_log_dt, s5_b_re, s5_b_im, s5_c_re, s5_c_im, s5_d, glu_w, glu_b, w_out, ln1_g, ln1_b, ffn_w1, ffn_w3, ffn_w2, ln2_g, ln2_b):
    p = _prepare(ln_in_g, ln_in_b, w_in, mu_shift, w0, w_lora_up, a0, a_lora_up, g_lora_up, k_k,
                 k_a, r_k, gn_g, gn_b, s5_a_re, s5_a_im, s5_log_dt, s5_b_re, s5_b_im, s5_c_re,
                 s5_c_im, s5_d, glu_w, glu_b, w_out, ln1_g, ln1_b, ffn_w1, ffn_w3, ffn_w2, ln2_g,
                 ln2_b)
    bp = x_prompt.shape[0]
    dt = x_prompt.dtype
    zeros = lambda *s: jnp.zeros((1, bp) + s, dt)
    yp, sh_p, wkv_p, re_p, im_p = _run(
        x_prompt, zeros(1, N_SHIFT), zeros(N_HEADS, HEAD, HEAD), zeros(S5_GROUPS, S5_STATE),
        zeros(S5_GROUPS, S5_STATE), p)
    ys, sh_s, wkv_s, re_s, im_s = _run(x_sample, cache_shift, state_wkv, state_s5_re, state_s5_im, p)
    return (yp, ys, sh_p, wkv_p, re_p, im_p, sh_s, wkv_s, re_s, im_s)
```

```python
import functools
import math

import jax
import jax.numpy as jnp
from jax import lax
from jax.experimental import pallas as pl
from jax.experimental.pallas import tpu as pltpu

F32 = jnp.float32
BF16 = jnp.bfloat16

D_MODEL = 2048
D_RWKV = 1024
D_S5 = 1024
HEAD = 64
N_HEADS = D_RWKV // HEAD
LORA_W = 64
LORA_A = 64
LORA_G = 160
N_LORA = LORA_W + LORA_A + LORA_G
N_LORA_PAD = 512
N_SHIFT = 3 * D_RWKV + N_LORA
N_SHIFT_PAD = 3 * D_RWKV + N_LORA_PAD
N_PROJ_PAD = 3 * D_RWKV + D_S5 + N_LORA_PAD
COL_U = 3
COL_LORA = 8
S5_GROUP = 16
S5_GROUPS = D_S5 // S5_GROUP
S5_STATE = 64
S5_WIDE = S5_GROUPS * S5_STATE
S5_BUNDLE = 8
CHUNK = 64
LN_EPS = 1e-5
GN_EPS = 64e-5
ALPHA = 2.0 ** 0.25
VMEM_LIMIT = 56 * 1024 * 1024
_LN_ROWS = 256


def _split(x):
    hi = x.astype(BF16)
    lo = (x - hi.astype(F32)).astype(BF16)
    return hi, lo


def _dot(a, b, dims=((1,), (0,))):
    return lax.dot_general(a, b, (dims, ((), ())), preferred_element_type=F32)


def _mm(a, b, dims=((1,), (0,)), passes=1):
    if passes == 1:
        return _dot(a.astype(BF16), b.astype(BF16), dims)
    ah, al = _split(a)
    if passes == 2:
        bh = b.astype(BF16)
        return _dot(ah, bh, dims) + _dot(al, bh, dims)
    bh, bl = _split(b)
    return _dot(ah, bh, dims) + (_dot(al, bh, dims) + _dot(ah, bl, dims))


def _layer_norm(x, g, b):
    mu = jnp.mean(x, axis=-1, keepdims=True)
    xc = x - mu
    var = jnp.mean(xc * xc, axis=-1, keepdims=True)
    return xc * lax.rsqrt(var + LN_EPS) * g + b


def _row_tile(t, cap):
    for c in (1024, 768, 512, 256, 128, 64):
        if c <= cap and t % c == 0:
            return c
    raise ValueError(f"token count {t} is not a multiple of 64")


def _const_spec(arr):
    nd = arr.ndim
    return pl.BlockSpec(arr.shape, lambda *_: (0,) * nd)


def _ln_proj_kernel(x_ref, g_ref, b_ref, w_ref, o_ref):
    for r in range(0, x_ref.shape[0], _LN_ROWS):
        rows = slice(r, r + _LN_ROWS)
        xn = _layer_norm(x_ref[rows, :], g_ref[...], b_ref[...]).astype(BF16)
        o_ref[rows, :] = _dot(xn, w_ref[...])


def _ln_proj(x2d, g, b, w_bf16):
    t, d = x2d.shape
    n = w_bf16.shape[1]
    tm = _row_tile(t, 1024)
    tn = n // 3
    return pl.pallas_call(
        _ln_proj_kernel,
        grid=(t // tm, n // tn),
        in_specs=[
            pl.BlockSpec((tm, d), lambda i, j: (i, 0)),
            pl.BlockSpec((1, d), lambda i, j: (0, 0)),
            pl.BlockSpec((1, d), lambda i, j: (0, 0)),
            pl.BlockSpec((d, tn), lambda i, j: (0, j)),
        ],
        out_specs=pl.BlockSpec((tm, tn), lambda i, j: (i, j)),
        out_shape=jax.ShapeDtypeStruct((t, n), F32),
        compiler_params=pltpu.CompilerParams(
            dimension_semantics=("parallel", "arbitrary"), vmem_limit_bytes=VMEM_LIMIT),
    )(x2d, g, b, w_bf16)


def _segsum(x, ones):
    xb = x.astype(BF16)
    return jnp.concatenate(
        [_dot(xb[:, 256 * q:256 * (q + 1)], ones) for q in range(x.shape[1] // 256)], axis=1)


def _rwkv_kernel(pr_ref, pk_ref, pv_ref, plo_ref, sh_ref, s0_ref, mu_ref, w0_ref, wup_ref,
                 a0_ref, aup_ref, gup_ref, kkw_ref, kaw_ref, rkw_ref, gng_ref, gnb_ref,
                 tri_ref, ones_ref,
                 o_ref, st_ref,
                 carry_ref, r_s, k_s, v_s, kk_s, b_s, ld_s, lc_s, g_s, y_s,
                 *, ns, tb):
    @pl.when(pl.program_id(1) == 0)
    def _():
        carry_ref[...] = sh_ref[...]
        st_ref[...] = s0_ref[...]

    row = lax.broadcasted_iota(jnp.int32, (tb, 1), 0)

    def token_shift(x_ref, c0):
        width = x_ref.shape[-1]
        outs = []
        for s in range(ns):
            x = x_ref[s]
            prev = jnp.where(row == 0, carry_ref[s, :, c0:c0 + width], pltpu.roll(x, 1, 0))
            outs.append(x + (prev - x) * mu_ref[:, c0:c0 + width])
            carry_ref[s, :, c0:c0 + width] = x[tb - 1:tb, :]
        return jnp.concatenate(outs, axis=0)

    ones = ones_ref[...]
    r = token_shift(pr_ref, 0)
    k = token_shift(pk_ref, D_RWKV)
    v = token_shift(pv_ref, 2 * D_RWKV)
    lo = token_shift(plo_ref, 3 * D_RWKV)

    lane = lax.broadcasted_iota(jnp.int32, (ns * tb, 128), 1)
    lo_wa = lo[:, 0:128]
    act_wa = jnp.where(lane < LORA_W, jnp.tanh(lo_wa), lo_wa).astype(BF16)
    act_g = jax.nn.sigmoid(lo[:, 128:384]).astype(BF16)
    w = w0_ref[...] + _dot(act_wa, wup_ref[...])
    ld = (-math.exp(-0.5)) * jax.nn.sigmoid(w)
    a = jax.nn.sigmoid(a0_ref[...] + _dot(act_wa, aup_ref[...]))
    g_s[...] = _dot(act_g, gup_ref[...])

    kk = k * kkw_ref[...]
    kk = kk * jnp.minimum(lax.rsqrt(_segsum(kk * kk, ones)), 1e12)
    k = k * (1.0 + (a - 1.0) * kaw_ref[...])

    tri = tri_ref[...]
    h1 = ld.astype(BF16)
    r1 = ld - h1.astype(F32)
    h2 = r1.astype(BF16)
    h3 = (r1 - h2.astype(F32)).astype(BF16)
    for s in range(ns):
        rows = slice(s * tb, (s + 1) * tb)
        lc_s[rows, :] = _dot(tri, h1[rows]) + (_dot(tri, h2[rows]) + _dot(tri, h3[rows]))
    ld_s[...] = ld
    r_s[...] = r
    k_s[...] = k
    v_s[...] = v
    kk_s[...] = kk
    b_s[...] = kk * a

    rr = lax.broadcasted_iota(jnp.int32, (2 * CHUNK, 2 * CHUNK), 0)
    cc = lax.broadcasted_iota(jnp.int32, (2 * CHUNK, 2 * CHUNK), 1)
    causal = cc % CHUNK < rr % CHUNK + jnp.where(rr < CHUNK, 1, 0)
    t_i = lax.broadcasted_iota(jnp.int32, (CHUNK, CHUNK), 0)
    s_i = lax.broadcasted_iota(jnp.int32, (CHUNK, CHUNK), 1)

    def same_block(size):
        return t_i // size == s_i // size

    diag_mask = jnp.where(same_block(_SOLVE_BASE), 1.0, 0.0).astype(F32)
    merge_masks = []
    size = _SOLVE_BASE
    while size < CHUNK:
        merge_masks.append(jnp.where(
            same_block(2 * size) & jnp.logical_not(same_block(size)), 1.0, 0.0).astype(F32))
        size *= 2
    lane_z = lax.broadcasted_iota(jnp.int32, (CHUNK, 4 * HEAD), 1)
    eye = jnp.where(t_i == s_i, 1.0, 0.0).astype(F32)

    def chunk_body(c, _):
        c0 = pl.multiple_of(c * CHUNK, CHUNK)
        units = []
        per_seq = []
        for s in range(ns):
            sl = pl.ds(s * tb + c0, CHUNK)
            lc = lc_s[sl, :]
            e_inc = jnp.exp(lc)
            e_exc = jnp.exp(lc - ld_s[sl, :])
            e_inv = jnp.exp(-lc)
            lc_end = lc[CHUNK - 1:CHUNK, :]
            e_hat = jnp.exp(lc_end - lc)
            kc = k_s[sl, :]
            bc = b_s[sl, :]
            per_seq.append(dict(
                sl=sl, e_end=jnp.exp(lc_end), vc=v_s[sl, :], rt=r_s[sl, :] * e_inc,
                kap=kk_s[sl, :] * e_exc, kt=kc * e_inv, bt=bc * e_inv, khat=kc * e_hat,
                bhat=bc * e_hat))
            units += [(s, h, slice(HEAD * h, HEAD * (h + 1))) for h in range(N_HEADS)]

        gms = []
        for s, h, hs in units:
            q = per_seq[s]
            lhs = jnp.concatenate([q["rt"][:, hs], q["kap"][:, hs]], axis=0)
            rhs = jnp.concatenate([q["kt"][:, hs], q["bt"][:, hs]], axis=0)
            gms.append(jnp.where(causal, _mm(lhs, rhs, ((1,), (1,))), 0.0))
        nmats = [-gm[CHUNK:, HEAD:2 * HEAD] for gm in gms]
        zeds = []
        for (s, h, hs), gm, nm in zip(units, gms, nmats):
            q = per_seq[s]
            akk_v = _mm(gm[CHUNK:, 0:HEAD], q["vc"][:, hs])
            zeds.append(jnp.concatenate(
                [nm * diag_mask, eye, akk_v, q["kap"][:, hs]], axis=1))
        steps = _SOLVE_BASE.bit_length() - 1
        for it in range(steps):
            ress = [_mm(zed[:, 0:HEAD], zed) for zed in zeds]
            if it < steps - 1:
                zeds = [jnp.where(lane_z < HEAD, res, zed + res) for zed, res in zip(zeds, ress)]
            else:
                zeds = [zed + res for zed, res in zip(zeds, ress)]
        for mask in merge_masks:
            ress = [_mm(nm * mask, zed) for nm, zed in zip(nmats, zeds)]
            ress = [_mm(zed[:, HEAD:2 * HEAD], res) for zed, res in zip(zeds, ress)]
            zeds = [zed + res for zed, res in zip(zeds, ress)]
        s_hs = [st_ref[s, h] for s, h, _ in units]
        ds = [_mm(jnp.concatenate([per_seq[s]["rt"][:, hs], zed[:, 3 * HEAD:4 * HEAD]], axis=0),
                  s_h, ((1,), (1,)))
              for (s, h, hs), zed, s_h in zip(units, zeds, s_hs)]
        vus = [jnp.concatenate([per_seq[s]["vc"][:, hs],
                                -(zed[:, 2 * HEAD:3 * HEAD] + d[CHUNK:, :])], axis=0)
               for (s, h, hs), zed, d in zip(units, zeds, ds)]
        for (s, h, hs), gm, d, vu in zip(units, gms, ds, vus):
            y_s[per_seq[s]["sl"], hs] = d[:CHUNK, :] + _mm(gm[:CHUNK, :], vu)
        for (s, h, hs), s_h, vu in zip(units, s_hs, vus):
            q = per_seq[s]
            kb_hat = jnp.concatenate([q["khat"][:, hs], q["bhat"][:, hs]], axis=0)
            st_ref[s, h] = s_h * q["e_end"][:, hs] + _mm(vu, kb_hat, ((0,), (0,)))
        return 0

    lax.fori_loop(0, tb // CHUNK, chunk_body, 0)

    y = y_s[...]
    mu = _segsum(y, ones) * (1.0 / HEAD)
    yc = y - mu
    var = _segsum(yc * yc, ones) * (1.0 / HEAD)
    yn = yc * lax.rsqrt(var + GN_EPS) * gng_ref[...] + gnb_ref[...]
    bonus = _segsum(r_s[...] * k_s[...] * rkw_ref[...], ones) * v_s[...]
    o_ref[...] = ((yn + bonus) * g_s[...]).astype(o_ref.dtype).reshape(ns, tb, D_RWKV)


_SOLVE_BASE = 8
_RWKV_SEQS = 2


def _rwkv(proj3, shift0p, wkv0, p):
    bsz, seq, _ = proj3.shape
    ns = _RWKV_SEQS
    assert bsz % ns == 0
    tb = 256 if seq % 256 == 0 else CHUNK
    nl = seq // tb

    def col(cb, width=D_RWKV):
        return pl.BlockSpec((ns, tb, width), lambda b, l, cb=cb: (b, l, cb))

    consts = [p["mu_p"], p["w0"], p["wup"], p["a0"], p["aup"], p["gup"], p["k_k"], p["k_a"],
              p["r_k"], p["gn_g"], p["gn_b"], p["tri"][:tb, :tb], p["ones"]]
    state_spec = pl.BlockSpec((ns, N_HEADS, HEAD, HEAD), lambda b, l: (b, 0, 0, 0))
    scratch = ([pltpu.VMEM((ns, 1, N_SHIFT_PAD), F32)]
               + [pltpu.VMEM((ns * tb, D_RWKV), F32)] * 9)
    return pl.pallas_call(
        functools.partial(_rwkv_kernel, ns=ns, tb=tb),
        grid=(bsz // ns, nl),
        in_specs=[col(0), col(1), col(2), col(COL_LORA, N_LORA_PAD),
                  pl.BlockSpec((ns, 1, N_SHIFT_PAD), lambda b, l: (b, 0, 0)),
                  state_spec] + [_const_spec(c) for c in consts],
        out_specs=[pl.BlockSpec((ns, tb, D_RWKV), lambda b, l: (b, l, 0)), state_spec],
        out_shape=[jax.ShapeDtypeStruct((bsz, seq, D_RWKV), BF16),
                   jax.ShapeDtypeStruct((bsz, N_HEADS, HEAD, HEAD), F32)],
        scratch_shapes=scratch,
        compiler_params=pltpu.CompilerParams(
            dimension_semantics=("parallel", "arbitrary"), vmem_limit_bytes=VMEM_LIMIT),
    )(proj3, proj3, proj3, proj3, shift0p, wkv0, *consts)


def _s5_kernel(u_ref, hre0_ref, him0_ref, lre_ref, lim_ref, wbre_ref, wbim_ref, wcre_ref,
               wcim_ref, d_ref, gluw_ref, glub_ref,
               o_ref, hre_ref, him_ref,
               *scratch, ns, tb):
    xre, xim = scratch[:ns], scratch[ns:]
    n_col = S5_WIDE // 128
    per_bundle = S5_BUNDLE * S5_STATE // 128

    @pl.when(pl.program_id(1) == 0)
    def _():
        hre_ref[...] = hre0_ref[...]
        him_ref[...] = him0_ref[...]

    kin = S5_BUNDLE * S5_GROUP
    lre = lre_ref[...]
    lim = lim_ref[...]

    def col(j):
        return pl.ds(j, tb, stride=_S5_TOK_ROWS)

    def in_map(s):
        u = u_ref[s]
        for m in range(S5_GROUPS // S5_BUNDLE):
            um = u[:, kin * m:kin * (m + 1)].astype(BF16)
            res_re = _dot(um, wbre_ref[m])
            res_im = _dot(um, wbim_ref[m])
            for c in range(per_bundle):
                xre[s][col(per_bundle * m + c), :] = res_re[:, 128 * c:128 * (c + 1)]
                xim[s][col(per_bundle * m + c), :] = res_im[:, 128 * c:128 * (c + 1)]

    def scan(seqs):
        state = {s: (hre_ref[s], him_ref[s]) for s in seqs}
        for t in range(tb):
            rows = slice(t * _S5_TOK_ROWS, t * _S5_TOK_ROWS + n_col)
            for s in seqs:
                hre, him = state[s]
                hre, him = (lre * hre - lim * him + xre[s][rows, :],
                            lre * him + lim * hre + xim[s][rows, :])
                xre[s][rows, :] = hre
                xim[s][rows, :] = him
                state[s] = (hre, him)
        for s in seqs:
            hre_ref[s], him_ref[s] = state[s]

    def out_map(s):
        ys = []
        for m in range(S5_GROUPS // S5_BUNDLE):
            cols = range(per_bundle * m, per_bundle * (m + 1))
            h_re = jnp.concatenate([xre[s][col(j), :] for j in cols], axis=1).astype(BF16)
            h_im = jnp.concatenate([xim[s][col(j), :] for j in cols], axis=1).astype(BF16)
            ys.append(_dot(h_re, wcre_ref[m]) + _dot(h_im, wcim_ref[m]))
        y = jnp.concatenate(ys, axis=1) + d_ref[...] * u_ref[s]
        zg = 0.5 * y * (1.0 + jnp.tanh(math.sqrt(2.0 / math.pi) * (y + 0.044715 * (y * y * y))))
        gate = jax.nn.sigmoid(_dot(zg.astype(BF16), gluw_ref[...]) + glub_ref[...])
        o_ref[s] = (zg * gate).astype(o_ref.dtype)

    groups = [list(range(g, min(g + _S5_GROUP_SEQS, ns))) for g in range(0, ns, _S5_GROUP_SEQS)]
    for s in groups[0]:
        in_map(s)
    for gi, group in enumerate(groups):
        if gi + 1 < len(groups):
            for s in groups[gi + 1]:
                in_map(s)
        scan(group)
        if gi > 0:
            for s in groups[gi - 1]:
                out_map(s)
    for s in groups[-1]:
        out_map(s)


_S5_SEQS = 4
_S5_GROUP_SEQS = 4
_S5_TOK_ROWS = 40


def _s5(proj3, hre0, him0, p):
    bsz, seq, _ = proj3.shape
    ns = _S5_SEQS
    assert bsz % ns == 0
    tb = 128 if seq % 128 == 0 else CHUNK
    nl = seq // tb
    consts = [p["lam_re"], p["lam_im"], p["wb_re"], p["wb_im"], p["wc_re"], p["wc_im"],
              p["s5_d"], p["glu_w"], p["glu_b"]]
    state_spec = pl.BlockSpec((ns, S5_WIDE // 128, 128), lambda b, l: (b, 0, 0))
    return pl.pallas_call(
        functools.partial(_s5_kernel, ns=ns, tb=tb),
        grid=(bsz // ns, nl),
        in_specs=[pl.BlockSpec((ns, tb, D_S5), lambda b, l: (b, l, COL_U)),
                  state_spec, state_spec] + [_const_spec(c) for c in consts],
        out_specs=[pl.BlockSpec((ns, tb, D_S5), lambda b, l: (b, l, 0)), state_spec, state_spec],
        out_shape=[jax.ShapeDtypeStruct((bsz, seq, D_S5), BF16),
                   jax.ShapeDtypeStruct((bsz, S5_WIDE // 128, 128), F32),
                   jax.ShapeDtypeStruct((bsz, S5_WIDE // 128, 128), F32)],
        scratch_shapes=[pltpu.VMEM((tb * _S5_TOK_ROWS, 128), F32)] * (2 * ns),
        compiler_params=pltpu.CompilerParams(
            dimension_semantics=("parallel", "arbitrary"), vmem_limit_bytes=VMEM_LIMIT),
    )(proj3, hre0, him0, *consts)


def _out_ln_kernel(x_ref, orw_ref, os5_ref, ging_ref, binb_ref, wa_ref, wb_ref, g_ref, b_ref, o_ref):
    for r in range(0, x_ref.shape[0], _LN_ROWS):
        rows = slice(r, r + _LN_ROWS)
        xn = _layer_norm(x_ref[rows, :], ging_ref[...], binb_ref[...])
        mix = _dot(orw_ref[rows, :], wa_ref[...]) + _dot(os5_ref[rows, :], wb_ref[...])
        o_ref[rows, :] = _layer_norm(ALPHA * xn + mix, g_ref[...], b_ref[...])


def _out_ln(x2d, o_rw, o_s5, p):
    t, d = x2d.shape
    tm = _row_tile(t, 512)
    consts = [p["ln_in_g"], p["ln_in_b"], p["w_out_a"], p["w_out_b"], p["ln1_g"], p["ln1_b"]]
    return pl.pallas_call(
        _out_ln_kernel,
        grid=(t // tm,),
        in_specs=[pl.BlockSpec((tm, d), lambda i: (i, 0)),
                  pl.BlockSpec((tm, D_RWKV), lambda i: (i, 0)),
                  pl.BlockSpec((tm, D_S5), lambda i: (i, 0))] + [_const_spec(c) for c in consts],
        out_specs=pl.BlockSpec((tm, d), lambda i: (i, 0)),
        out_shape=jax.ShapeDtypeStruct((t, d), F32),
        compiler_params=pltpu.CompilerParams(
            dimension_semantics=("parallel",), vmem_limit_bytes=VMEM_LIMIT),
    )(x2d, o_rw, o_s5, *consts)


def _ffn_ln_kernel(x_ref, w1_ref, w3_ref, w2_ref, g_ref, b_ref, o_ref, xb_ref, *, nf):
    f = pl.program_id(1)

    @pl.when(f == 0)
    def _():
        xb_ref[...] = x_ref[...].astype(BF16)
        o_ref[...] = jnp.zeros_like(o_ref)

    xb = xb_ref[...]
    h1 = _dot(xb, w1_ref[...])
    h3 = _dot(xb, w3_ref[...])
    act = (h1 * jax.nn.sigmoid(h1) * h3).astype(BF16)
    o_ref[...] += _dot(act, w2_ref[...])

    @pl.when(f == nf - 1)
    def _():
        o_ref[...] = _layer_norm(ALPHA * x_ref[...] + o_ref[...], g_ref[...], b_ref[...])


def _ffn_ln(x1, p):
    t, d = x1.shape
    dff = p["ffn_w1"].shape[1]
    tm = _row_tile(t, 512)
    tf = 512
    nf = dff // tf
    return pl.pallas_call(
        functools.partial(_ffn_ln_kernel, nf=nf),
        grid=(t // tm, nf),
        in_specs=[pl.BlockSpec((tm, d), lambda i, f: (i, 0)),
                  pl.BlockSpec((d, tf), lambda i, f: (0, f)),
                  pl.BlockSpec((d, tf), lambda i, f: (0, f)),
                  pl.BlockSpec((tf, d), lambda i, f: (f, 0)),
                  pl.BlockSpec((1, d), lambda i, f: (0, 0)),
                  pl.BlockSpec((1, d), lambda i, f: (0, 0))],
        out_specs=pl.BlockSpec((tm, d), lambda i, f: (i, 0)),
        out_shape=jax.ShapeDtypeStruct((t, d), F32),
        scratch_shapes=[pltpu.VMEM((tm, d), BF16)],
        compiler_params=pltpu.CompilerParams(
            dimension_semantics=("parallel", "arbitrary"), vmem_limit_bytes=VMEM_LIMIT),
    )(x1, p["ffn_w1"], p["ffn_w3"], p["ffn_w2"], p["ln2_g"], p["ln2_b"])


def _pad_cols(x, n):
    return jnp.pad(x, [(0, 0)] * (x.ndim - 1) + [(0, n - x.shape[-1])])


def _shift_layout(x):
    return jnp.concatenate([x[..., :3 * D_RWKV], _pad_cols(x[..., 3 * D_RWKV:], N_LORA_PAD)], axis=-1)


def _block_diag(blocks):
    nb, g, a, b = blocks.shape
    eye = jnp.eye(g, dtype=blocks.dtype)
    return jnp.einsum("mgab,gh->mgahb", blocks, eye).reshape(nb, g * a, g * b)


def _prepare(ln_in_g, ln_in_b, w_in, mu_shift, w0, w_lora_up, a0, a_lora_up, g_lora_up, k_k, k_a,
             r_k, gn_g, gn_b, s5_a_re, s5_a_im, s5_log_dt, s5_b_re, s5_b_im, s5_c_re, s5_c_im,
             s5_d, glu_w, glu_b, w_out, ln1_g, ln1_b, ffn_w1, ffn_w3, ffn_w2, ln2_g, ln2_b):
    row = lambda x: x.reshape(1, -1).astype(F32)
    w = w_in[0]
    o3 = 3 * D_RWKV
    w_in_p = jnp.concatenate(
        [w[:, :o3], w[:, N_SHIFT:], _pad_cols(w[:, o3:N_SHIFT], N_LORA_PAD)], axis=1).astype(BF16)
    wup = jnp.zeros((128, D_RWKV), F32).at[:LORA_W].set(w_lora_up[0]).astype(BF16)
    aup = jnp.zeros((128, D_RWKV), F32).at[LORA_W:LORA_W + LORA_A].set(a_lora_up[0]).astype(BF16)
    gup = jnp.zeros((256, D_RWKV), F32).at[:LORA_G].set(g_lora_up[0]).astype(BF16)
    idx = jnp.arange(256)
    tri = ((idx[:, None] >= idx[None, :]) & (idx[:, None] // CHUNK == idx[None, :] // CHUNK)).astype(BF16)
    ones = (idx[:, None] // HEAD == idx[None, :] // HEAD).astype(BF16)

    dt = jnp.exp(s5_log_dt[0].astype(F32))[:, None]
    a_re, a_im = s5_a_re[0].astype(F32), s5_a_im[0].astype(F32)
    mag = jnp.exp(a_re * dt)
    lam_re, lam_im = mag * jnp.cos(a_im * dt), mag * jnp.sin(a_im * dt)
    den = a_re * a_re + a_im * a_im
    q_re = ((lam_re - 1.0) * a_re + lam_im * a_im) / den
    q_im = (lam_im * a_re - (lam_re - 1.0) * a_im) / den
    b_re, b_im = s5_b_re[0].astype(F32), s5_b_im[0].astype(F32)
    bb_re = q_re[..., None] * b_re - q_im[..., None] * b_im
    bb_im = q_re[..., None] * b_im + q_im[..., None] * b_re
    nb = S5_GROUPS // S5_BUNDLE
    to_in = lambda x: _block_diag(jnp.swapaxes(x, 1, 2).reshape(nb, S5_BUNDLE, S5_GROUP, S5_STATE))
    to_out = lambda x: _block_diag(jnp.swapaxes(x, 1, 2).reshape(nb, S5_BUNDLE, S5_STATE, S5_GROUP))
    return {
        "ln_in_g": row(ln_in_g), "ln_in_b": row(ln_in_b), "w_in_p": w_in_p,
        "mu_p": _shift_layout(row(mu_shift[0])), "w0": row(w0[0]), "wup": wup, "a0": row(a0[0]),
        "aup": aup, "gup": gup, "k_k": row(k_k[0]), "k_a": row(k_a[0]), "r_k": row(r_k[0]),
        "gn_g": row(gn_g[0]), "gn_b": row(gn_b[0]), "tri": tri, "ones": ones,
        "lam_re": lam_re.reshape(-1, 128), "lam_im": lam_im.reshape(-1, 128),
        "wb_re": to_in(bb_re).astype(BF16), "wb_im": to_in(bb_im).astype(BF16),
        "wc_re": to_out(s5_c_re[0].astype(F32)).astype(BF16),
        "wc_im": to_out(-s5_c_im[0].astype(F32)).astype(BF16),
        "s5_d": row(s5_d[0]), "glu_w": glu_w[0].astype(BF16), "glu_b": row(glu_b[0]),
        "w_out_a": w_out[0, :D_RWKV].astype(BF16), "w_out_b": w_out[0, D_RWKV:].astype(BF16),
        "ln1_g": row(ln1_g[0]), "ln1_b": row(ln1_b[0]),
        "ffn_w1": ffn_w1[0].astype(BF16), "ffn_w3": ffn_w3[0].astype(BF16),
        "ffn_w2": ffn_w2[0].astype(BF16), "ln2_g": row(ln2_g[0]), "ln2_b": row(ln2_b[0]),
    }


def _run(x, shift, wkv, h_re, h_im, p):
    bsz, seq, d = x.shape
    x2d = x.reshape(bsz * seq, d)
    proj = _ln_proj(x2d, p["ln_in_g"], p["ln_in_b"], p["w_in_p"])
    proj3 = proj.reshape(bsz, seq, N_PROJ_PAD)
    o_rw, n_wkv = _rwkv(proj3, _shift_layout(shift[0]), wkv[0], p)
    o_s5, n_hre, n_him = _s5(proj3, h_re[0].reshape(bsz, S5_WIDE // 128, 128),
                             h_im[0].reshape(bsz, S5_WIDE // 128, 128), p)
    x1 = _out_ln(x2d, o_rw.reshape(bsz * seq, D_RWKV), o_s5.reshape(bsz * seq, D_S5), p)
    y = _ffn_ln(x1, p).reshape(bsz, seq, d)
    last = proj3[:, seq - 1:, :]
    lo0 = COL_LORA * N_LORA_PAD
    n_shift = jnp.concatenate([last[..., :3 * D_RWKV], last[..., lo0:lo0 + N_LORA]], axis=-1)
    return (y, n_shift[None], n_wkv[None],
            n_hre.reshape(1, bsz, S5_GROUPS, S5_STATE), n_him.reshape(1, bsz, S5_GROUPS, S5_STATE))


def kernel(x_prompt, x_sample, cache_shift, state_wkv, state_s5_re, state_s5_im, ln_in_g, ln_in_b, w_in, mu_shift, w0, w_lora_up, a0, a_lora_up, g_lora_up, k_k, k_a, r_k, gn_g, gn_b, s5_a_re, s5_a_im, s5_log_dt, s5_b_re, s5_b_im, s5_c_re, s5_c_im, s5_d, glu_w, glu_b, w_out, ln1_g, ln1_b, ffn_w1, ffn_w3, ffn_w2, ln2_g, ln2_b):
    p = _prepare(ln_in_g, ln_in_b, w_in, mu_shift, w0, w_lora_up, a0, a_lora_up, g_lora_up, k_k,
                 k_a, r_k, gn_g, gn_b, s5_a_re, s5_a_im, s5_log_dt, s5_b_re, s5_b_im, s5_c_re,
                 s5_c_im, s5_d, glu_w, glu_b, w_out, ln1_g, ln1_b, ffn_w1, ffn_w3, ffn_w2, ln2_g,
                 ln2_b)
    bp = x_prompt.shape[0]
    dt = x_prompt.dtype
    zeros = lambda *s: jnp.zeros((1, bp) + s, dt)
    yp, sh_p, wkv_p, re_p, im_p = _run(
        x_prompt, zeros(1, N_SHIFT), zeros(N_HEADS, HEAD, HEAD), zeros(S5_GROUPS, S5_STATE),
        zeros(S5_GROUPS, S5_STATE), p)
    ys, sh_s, wkv_s, re_s, im_s = _run(x_sample, cache_shift, state_wkv, state_s5_re, state_s5_im, p)
    return (yp, ys, sh_p, wkv_p, re_p, im_p, sh_s, wkv_s, re_s, im_s)
```

```python
import functools
import math

import jax
import jax.numpy as jnp
from jax import lax
from jax.experimental import pallas as pl
from jax.experimental.pallas import tpu as pltpu

F32 = jnp.float32
BF16 = jnp.bfloat16

D_MODEL = 2048
D_RWKV = 1024
D_S5 = 1024
HEAD = 64
N_HEADS = D_RWKV // HEAD
LORA_W = 64
LORA_A = 64
LORA_G = 160
N_LORA = LORA_W + LORA_A + LORA_G
N_LORA_PAD = 512
N_SHIFT = 3 * D_RWKV + N_LORA
N_SHIFT_PAD = 3 * D_RWKV + N_LORA_PAD
N_PROJ_PAD = 3 * D_RWKV + D_S5 + N_LORA_PAD
COL_U = 3
COL_LORA = 8
S5_GROUP = 16
S5_GROUPS = D_S5 // S5_GROUP
S5_STATE = 64
S5_WIDE = S5_GROUPS * S5_STATE
S5_BUNDLE = 8
CHUNK = 64
LN_EPS = 1e-5
GN_EPS = 64e-5
ALPHA = 2.0 ** 0.25
VMEM_LIMIT = 56 * 1024 * 1024
_LN_ROWS = 256
_FFN_TF = 512


def _split(x):
    hi = x.astype(BF16)
    lo = (x - hi.astype(F32)).astype(BF16)
    return hi, lo


def _dot(a, b, dims=((1,), (0,))):
    return lax.dot_general(a, b, (dims, ((), ())), preferred_element_type=F32)


def _mm(a, b, dims=((1,), (0,)), passes=1):
    if passes == 1:
        return _dot(a.astype(BF16), b.astype(BF16), dims)
    ah, al = _split(a)
    if passes == 2:
        bh = b.astype(BF16)
        return _dot(ah, bh, dims) + _dot(al, bh, dims)
    bh, bl = _split(b)
    return _dot(ah, bh, dims) + (_dot(al, bh, dims) + _dot(ah, bl, dims))


def _layer_norm(x, g, b):
    mu = jnp.mean(x, axis=-1, keepdims=True)
    xc = x - mu
    var = jnp.mean(xc * xc, axis=-1, keepdims=True)
    return xc * lax.rsqrt(var + LN_EPS) * g + b


def _row_tile(t, cap):
    for c in (1024, 768, 512, 256, 128, 64):
        if c <= cap and t % c == 0:
            return c
    raise ValueError(f"token count {t} is not a multiple of 64")


def _const_spec(arr):
    nd = arr.ndim
    return pl.BlockSpec(arr.shape, lambda *_: (0,) * nd)


def _ln_proj_kernel(x_ref, g_ref, b_ref, wa_ref, wb_ref, o_ref):
    def tile(w_ref):
        for r in range(0, x_ref.shape[0], _LN_ROWS):
            rows = slice(r, r + _LN_ROWS)
            xn = _layer_norm(x_ref[rows, :], g_ref[...], b_ref[...]).astype(BF16)
            o_ref[rows, :] = _dot(xn, w_ref[...])

    last = pl.num_programs(1) - 1

    @pl.when(pl.program_id(1) < last)
    def _():
        tile(wa_ref)

    @pl.when(pl.program_id(1) == last)
    def _():
        tile(wb_ref)


def _ln_proj(x2d, g, b, w_rkv, w_ul):
    t, d = x2d.shape
    tn = w_ul.shape[1]
    na = w_rkv.shape[1] // tn
    tm = _row_tile(t, 1024)
    return pl.pallas_call(
        _ln_proj_kernel,
        grid=(t // tm, na + 1),
        in_specs=[
            pl.BlockSpec((tm, d), lambda i, j: (i, 0)),
            pl.BlockSpec((1, d), lambda i, j: (0, 0)),
            pl.BlockSpec((1, d), lambda i, j: (0, 0)),
            pl.BlockSpec((d, tn), lambda i, j: (0, jnp.minimum(j, na - 1))),
            pl.BlockSpec((d, tn), lambda i, j: (0, 0)),
        ],
        out_specs=pl.BlockSpec((tm, tn), lambda i, j: (i, j)),
        out_shape=jax.ShapeDtypeStruct((t, (na + 1) * tn), F32),
        compiler_params=pltpu.CompilerParams(
            dimension_semantics=("parallel", "arbitrary"), vmem_limit_bytes=VMEM_LIMIT),
    )(x2d, g, b, w_rkv, w_ul)


def _segsum(x, ones):
    xb = x.astype(BF16)
    return jnp.concatenate(
        [_dot(xb[:, 256 * q:256 * (q + 1)], ones) for q in range(x.shape[1] // 256)], axis=1)


def _rwkv_kernel(pr_ref, pk_ref, pv_ref, plo_ref, sh_ref, s0_ref, mu_ref, w0_ref, wup_ref,
                 a0_ref, aup_ref, gup_ref, kkw_ref, kaw_ref, rkw_ref, gng_ref, gnb_ref,
                 tri_ref, ones_ref,
                 o_ref, st_ref,
                 carry_ref, r_s, k_s, v_s, kk_s, b_s, ld_s, lc_s, g_s, y_s,
                 *, ns, tb):
    @pl.when(pl.program_id(1) == 0)
    def _():
        carry_ref[...] = sh_ref[...]
        st_ref[...] = s0_ref[...]

    row = lax.broadcasted_iota(jnp.int32, (tb, 1), 0)

    def token_shift(x_ref, c0):
        width = x_ref.shape[-1]
        outs = []
        for s in range(ns):
            x = x_ref[s]
            prev = jnp.where(row == 0, carry_ref[s, :, c0:c0 + width], pltpu.roll(x, 1, 0))
            outs.append(x + (prev - x) * mu_ref[:, c0:c0 + width])
            carry_ref[s, :, c0:c0 + width] = x[tb - 1:tb, :]
        return jnp.concatenate(outs, axis=0)

    ones = ones_ref[...]
    r = token_shift(pr_ref, 0)
    k = token_shift(pk_ref, D_RWKV)
    v = token_shift(pv_ref, 2 * D_RWKV)
    lo = token_shift(plo_ref, 3 * D_RWKV)

    lane = lax.broadcasted_iota(jnp.int32, (ns * tb, 128), 1)
    lo_wa = lo[:, 0:128]
    act_wa = jnp.where(lane < LORA_W, jnp.tanh(lo_wa), lo_wa).astype(BF16)
    act_g = jax.nn.sigmoid(lo[:, 128:384]).astype(BF16)
    w = w0_ref[...] + _dot(act_wa, wup_ref[...])
    ld = (-math.exp(-0.5)) * jax.nn.sigmoid(w)
    a = jax.nn.sigmoid(a0_ref[...] + _dot(act_wa, aup_ref[...]))
    g_s[...] = _dot(act_g, gup_ref[...])

    kk = k * kkw_ref[...]
    kk = kk * jnp.minimum(lax.rsqrt(_segsum(kk * kk, ones)), 1e12)
    k = k * (1.0 + (a - 1.0) * kaw_ref[...])

    tri = tri_ref[...]
    h1 = ld.astype(BF16)
    r1 = ld - h1.astype(F32)
    h2 = r1.astype(BF16)
    h3 = (r1 - h2.astype(F32)).astype(BF16)
    for s in range(ns):
        rows = slice(s * tb, (s + 1) * tb)
        lc_s[rows, :] = _dot(tri, h1[rows]) + (_dot(tri, h2[rows]) + _dot(tri, h3[rows]))
    ld_s[...] = ld
    r_s[...] = r
    k_s[...] = k
    v_s[...] = v
    kk_s[...] = kk
    b_s[...] = kk * a

    rr = lax.broadcasted_iota(jnp.int32, (2 * CHUNK, 2 * CHUNK), 0)
    cc = lax.broadcasted_iota(jnp.int32, (2 * CHUNK, 2 * CHUNK), 1)
    causal = cc % CHUNK < rr % CHUNK + jnp.where(rr < CHUNK, 1, 0)
    t_i = lax.broadcasted_iota(jnp.int32, (CHUNK, CHUNK), 0)
    s_i = lax.broadcasted_iota(jnp.int32, (CHUNK, CHUNK), 1)

    def same_block(size):
        return t_i // size == s_i // size

    diag_mask = jnp.where(same_block(_SOLVE_BASE), 1.0, 0.0).astype(F32)
    merge_masks = []
    size = _SOLVE_BASE
    while size < CHUNK:
        merge_masks.append(jnp.where(
            same_block(2 * size) & jnp.logical_not(same_block(size)), 1.0, 0.0).astype(F32))
        size *= 2
    lane_z = lax.broadcasted_iota(jnp.int32, (CHUNK, 4 * HEAD), 1)
    eye = jnp.where(t_i == s_i, 1.0, 0.0).astype(F32)

    def chunk_body(c, _):
        c0 = pl.multiple_of(c * CHUNK, CHUNK)
        units = []
        per_seq = []
        for s in range(ns):
            sl = pl.ds(s * tb + c0, CHUNK)
            lc = lc_s[sl, :]
            e_inc = jnp.exp(lc)
            e_exc = jnp.exp(lc - ld_s[sl, :])
            e_inv = jnp.exp(-lc)
            lc_end = lc[CHUNK - 1:CHUNK, :]
            e_hat = jnp.exp(lc_end - lc)
            kc = k_s[sl, :]
            bc = b_s[sl, :]
            per_seq.append(dict(
                sl=sl, e_end=jnp.exp(lc_end), vc=v_s[sl, :], rt=r_s[sl, :] * e_inc,
                kap=kk_s[sl, :] * e_exc, kt=kc * e_inv, bt=bc * e_inv, khat=kc * e_hat,
                bhat=bc * e_hat))
            units += [(s, h, slice(HEAD * h, HEAD * (h + 1))) for h in range(N_HEADS)]

        gms = []
        for s, h, hs in units:
            q = per_seq[s]
            lhs = jnp.concatenate([q["rt"][:, hs], q["kap"][:, hs]], axis=0)
            rhs = jnp.concatenate([q["kt"][:, hs], q["bt"][:, hs]], axis=0)
            gms.append(jnp.where(causal, _mm(lhs, rhs, ((1,), (1,))), 0.0))
        nmats = [-gm[CHUNK:, HEAD:2 * HEAD] for gm in gms]
        zeds = []
        for (s, h, hs), gm, nm in zip(units, gms, nmats):
            q = per_seq[s]
            akk_v = _mm(gm[CHUNK:, 0:HEAD], q["vc"][:, hs])
            zeds.append(jnp.concatenate(
                [nm * diag_mask, eye, akk_v, q["kap"][:, hs]], axis=1))
        steps = _SOLVE_BASE.bit_length() - 1
        for it in range(steps):
            ress = [_mm(zed[:, 0:HEAD], zed) for zed in zeds]
            if it < steps - 1:
                zeds = [jnp.where(lane_z < HEAD, res, zed + res) for zed, res in zip(zeds, ress)]
            else:
                zeds = [zed + res for zed, res in zip(zeds, ress)]
        for mask in merge_masks:
            ress = [_mm(nm * mask, zed) for nm, zed in zip(nmats, zeds)]
            ress = [_mm(zed[:, HEAD:2 * HEAD], res) for zed, res in zip(zeds, ress)]
            zeds = [zed + res for zed, res in zip(zeds, ress)]
        s_hs = [st_ref[s, h] for s, h, _ in units]
        ds = [_mm(jnp.concatenate([per_seq[s]["rt"][:, hs], zed[:, 3 * HEAD:4 * HEAD]], axis=0),
                  s_h, ((1,), (1,)))
              for (s, h, hs), zed, s_h in zip(units, zeds, s_hs)]
        vus = [jnp.concatenate([per_seq[s]["vc"][:, hs],
                                -(zed[:, 2 * HEAD:3 * HEAD] + d[CHUNK:, :])], axis=0)
               for (s, h, hs), zed, d in zip(units, zeds, ds)]
        for (s, h, hs), gm, d, vu in zip(units, gms, ds, vus):
            y_s[per_seq[s]["sl"], hs] = d[:CHUNK, :] + _mm(gm[:CHUNK, :], vu)
        for (s, h, hs), s_h, vu in zip(units, s_hs, vus):
            q = per_seq[s]
            kb_hat = jnp.concatenate([q["khat"][:, hs], q["bhat"][:, hs]], axis=0)
            st_ref[s, h] = s_h * q["e_end"][:, hs] + _mm(vu, kb_hat, ((0,), (0,)))
        return 0

    lax.fori_loop(0, tb // CHUNK, chunk_body, 0)

    y = y_s[...]
    mu = _segsum(y, ones) * (1.0 / HEAD)
    yc = y - mu
    var = _segsum(yc * yc, ones) * (1.0 / HEAD)
    yn = yc * lax.rsqrt(var + GN_EPS) * gng_ref[...] + gnb_ref[...]
    bonus = _segsum(r_s[...] * k_s[...] * rkw_ref[...], ones) * v_s[...]
    o_ref[...] = ((yn + bonus) * g_s[...]).astype(o_ref.dtype).reshape(ns, tb, D_RWKV)


_SOLVE_BASE = 8
_RWKV_SEQS = 2


def _rwkv(proj3, shift0p, wkv0, p):
    bsz, seq, _ = proj3.shape
    tb = 256 if seq % 256 == 0 else CHUNK
    nl = seq // tb
    ns = _RWKV_SEQS if tb > CHUNK or bsz % (2 * _RWKV_SEQS) else 2 * _RWKV_SEQS
    assert bsz % ns == 0

    def col(cb, width=D_RWKV):
        return pl.BlockSpec((ns, tb, width), lambda b, l, cb=cb: (b, l, cb))

    consts = [p["mu_p"], p["w0"], p["wup"], p["a0"], p["aup"], p["gup"], p["k_k"], p["k_a"],
              p["r_k"], p["gn_g"], p["gn_b"], p["tri"][:tb, :tb], p["ones"]]
    state_spec = pl.BlockSpec((ns, N_HEADS, HEAD, HEAD), lambda b, l: (b, 0, 0, 0))
    scratch = ([pltpu.VMEM((ns, 1, N_SHIFT_PAD), F32)]
               + [pltpu.VMEM((ns * tb, D_RWKV), F32)] * 9)
    return pl.pallas_call(
        functools.partial(_rwkv_kernel, ns=ns, tb=tb),
        grid=(bsz // ns, nl),
        in_specs=[col(0), col(1), col(2), col(COL_LORA, N_LORA_PAD),
                  pl.BlockSpec((ns, 1, N_SHIFT_PAD), lambda b, l: (b, 0, 0)),
                  state_spec] + [_const_spec(c) for c in consts],
        out_specs=[pl.BlockSpec((ns, tb, D_RWKV), lambda b, l: (b, l, 0)), state_spec],
        out_shape=[jax.ShapeDtypeStruct((bsz, seq, D_RWKV), BF16),
                   jax.ShapeDtypeStruct((bsz, N_HEADS, HEAD, HEAD), F32)],
        scratch_shapes=scratch,
        compiler_params=pltpu.CompilerParams(
            dimension_semantics=("parallel", "arbitrary"), vmem_limit_bytes=VMEM_LIMIT),
    )(proj3, proj3, proj3, proj3, shift0p, wkv0, *consts)


def _s5_kernel(u_ref, hre0_ref, him0_ref, lre_ref, lim_ref, wbre_ref, wbim_ref, wcre_ref,
               wcim_ref, d_ref, gluw_ref, glub_ref,
               o_ref, hre_ref, him_ref,
               *scratch, ns, tb):
    xre, xim = scratch[:ns], scratch[ns:]
    n_col = S5_WIDE // 128
    per_bundle = S5_BUNDLE * S5_STATE // 128

    @pl.when(pl.program_id(1) == 0)
    def _():
        hre_ref[...] = hre0_ref[...]
        him_ref[...] = him0_ref[...]

    kin = S5_BUNDLE * S5_GROUP
    lre = lre_ref[...]
    lim = lim_ref[...]

    def col(j):
        return pl.ds(j, tb, stride=_S5_TOK_ROWS)

    def in_map(s):
        u = u_ref[s]
        for m in range(S5_GROUPS // S5_BUNDLE):
            um = u[:, kin * m:kin * (m + 1)].astype(BF16)
            res_re = _dot(um, wbre_ref[m])
            res_im = _dot(um, wbim_ref[m])
            for c in range(per_bundle):
                xre[s][col(per_bundle * m + c), :] = res_re[:, 128 * c:128 * (c + 1)]
                xim[s][col(per_bundle * m + c), :] = res_im[:, 128 * c:128 * (c + 1)]

    def scan(seqs):
        state = {s: (hre_ref[s], him_ref[s]) for s in seqs}
        for t in range(tb):
            rows = slice(t * _S5_TOK_ROWS, t * _S5_TOK_ROWS + n_col)
            for s in seqs:
                hre, him = state[s]
                hre, him = (lre * hre - lim * him + xre[s][rows, :],
                            lre * him + lim * hre + xim[s][rows, :])
                xre[s][rows, :] = hre
                xim[s][rows, :] = him
                state[s] = (hre, him)
        for s in seqs:
            hre_ref[s], him_ref[s] = state[s]

    def out_map(s):
        ys = []
        for m in range(S5_GROUPS // S5_BUNDLE):
            cols = range(per_bundle * m, per_bundle * (m + 1))
            h_re = jnp.concatenate([xre[s][col(j), :] for j in cols], axis=1).astype(BF16)
            h_im = jnp.concatenate([xim[s][col(j), :] for j in cols], axis=1).astype(BF16)
            ys.append(_dot(h_re, wcre_ref[m]) + _dot(h_im, wcim_ref[m]))
        y = jnp.concatenate(ys, axis=1) + d_ref[...] * u_ref[s]
        zg = 0.5 * y * (1.0 + jnp.tanh(math.sqrt(2.0 / math.pi) * (y + 0.044715 * (y * y * y))))
        gate = jax.nn.sigmoid(_dot(zg.astype(BF16), gluw_ref[...]) + glub_ref[...])
        o_ref[s] = (zg * gate).astype(o_ref.dtype)

    groups = [list(range(g, min(g + _S5_GROUP_SEQS, ns))) for g in range(0, ns, _S5_GROUP_SEQS)]
    for s in groups[0]:
        in_map(s)
    for gi, group in enumerate(groups):
        if gi + 1 < len(groups):
            for s in groups[gi + 1]:
                in_map(s)
        scan(group)
        if gi > 0:
            for s in groups[gi - 1]:
                out_map(s)
    for s in groups[-1]:
        out_map(s)


_S5_SEQS = 4
_S5_GROUP_SEQS = 4
_S5_TOK_ROWS = 40


def _s5(proj3, hre0, him0, p):
    bsz, seq, _ = proj3.shape
    ns = _S5_SEQS
    assert bsz % ns == 0
    tb = 128 if seq % 128 == 0 else CHUNK
    nl = seq // tb
    consts = [p["lam_re"], p["lam_im"], p["wb_re"], p["wb_im"], p["wc_re"], p["wc_im"],
              p["s5_d"], p["glu_w"], p["glu_b"]]
    state_spec = pl.BlockSpec((ns, S5_WIDE // 128, 128), lambda b, l: (b, 0, 0))
    return pl.pallas_call(
        functools.partial(_s5_kernel, ns=ns, tb=tb),
        grid=(bsz // ns, nl),
        in_specs=[pl.BlockSpec((ns, tb, D_S5), lambda b, l: (b, l, COL_U)),
                  state_spec, state_spec] + [_const_spec(c) for c in consts],
        out_specs=[pl.BlockSpec((ns, tb, D_S5), lambda b, l: (b, l, 0)), state_spec, state_spec],
        out_shape=[jax.ShapeDtypeStruct((bsz, seq, D_S5), BF16),
                   jax.ShapeDtypeStruct((bsz, S5_WIDE // 128, 128), F32),
                   jax.ShapeDtypeStruct((bsz, S5_WIDE // 128, 128), F32)],
        scratch_shapes=[pltpu.VMEM((tb * _S5_TOK_ROWS, 128), F32)] * (2 * ns),
        compiler_params=pltpu.CompilerParams(
            dimension_semantics=("parallel", "arbitrary"), vmem_limit_bytes=VMEM_LIMIT),
    )(proj3, hre0, him0, *consts)


def _out_ln_kernel(x_ref, orw_ref, os5_ref, ging_ref, binb_ref, wa_ref, wb_ref, g_ref, b_ref, o_ref):
    for r in range(0, x_ref.shape[0], _LN_ROWS):
        rows = slice(r, r + _LN_ROWS)
        xn = _layer_norm(x_ref[rows, :], ging_ref[...], binb_ref[...])
        mix = _dot(orw_ref[rows, :], wa_ref[...]) + _dot(os5_ref[rows, :], wb_ref[...])
        o_ref[rows, :] = _layer_norm(ALPHA * xn + mix, g_ref[...], b_ref[...])


def _out_ln(x2d, o_rw, o_s5, p):
    t, d = x2d.shape
    tm = _row_tile(t, 512)
    consts = [p["ln_in_g"], p["ln_in_b"], p["w_out_a"], p["w_out_b"], p["ln1_g"], p["ln1_b"]]
    return pl.pallas_call(
        _out_ln_kernel,
        grid=(t // tm,),
        in_specs=[pl.BlockSpec((tm, d), lambda i: (i, 0)),
                  pl.BlockSpec((tm, D_RWKV), lambda i: (i, 0)),
                  pl.BlockSpec((tm, D_S5), lambda i: (i, 0))] + [_const_spec(c) for c in consts],
        out_specs=pl.BlockSpec((tm, d), lambda i: (i, 0)),
        out_shape=jax.ShapeDtypeStruct((t, d), F32),
        compiler_params=pltpu.CompilerParams(
            dimension_semantics=("parallel",), vmem_limit_bytes=VMEM_LIMIT),
    )(x2d, o_rw, o_s5, *consts)


def _ffn_ln_kernel(x_ref, w13_ref, w2_ref, g_ref, b_ref, o_ref, xb_ref, *, nf):
    f = pl.program_id(1)

    @pl.when(f == 0)
    def _():
        xb_ref[...] = x_ref[...].astype(BF16)
        o_ref[...] = jnp.zeros_like(o_ref)

    h13 = _dot(xb_ref[...], w13_ref[...])
    h1 = h13[:, :_FFN_TF]
    h3 = h13[:, _FFN_TF:]
    act = (h1 * jax.nn.sigmoid(h1) * h3).astype(BF16)
    o_ref[...] += _dot(act, w2_ref[...])

    @pl.when(f == nf - 1)
    def _():
        o_ref[...] = _layer_norm(ALPHA * x_ref[...] + o_ref[...], g_ref[...], b_ref[...])


def _ffn_ln(x1, p):
    t, d = x1.shape
    dff = p["ffn_w2"].shape[0]
    tm = _row_tile(t, 512)
    tf = _FFN_TF
    nf = dff // tf
    return pl.pallas_call(
        functools.partial(_ffn_ln_kernel, nf=nf),
        grid=(t // tm, nf),
        in_specs=[pl.BlockSpec((tm, d), lambda i, f: (i, 0)),
                  pl.BlockSpec((d, 2 * tf), lambda i, f: (0, f)),
                  pl.BlockSpec((tf, d), lambda i, f: (f, 0)),
                  pl.BlockSpec((1, d), lambda i, f: (0, 0)),
                  pl.BlockSpec((1, d), lambda i, f: (0, 0))],
        out_specs=pl.BlockSpec((tm, d), lambda i, f: (i, 0)),
        out_shape=jax.ShapeDtypeStruct((t, d), F32),
        scratch_shapes=[pltpu.VMEM((tm, d), BF16)],
        compiler_params=pltpu.CompilerParams(
            dimension_semantics=("parallel", "arbitrary"), vmem_limit_bytes=VMEM_LIMIT),
    )(x1, p["ffn_w13"], p["ffn_w2"], p["ln2_g"], p["ln2_b"])


def _pad_cols(x, n):
    return jnp.pad(x, [(0, 0)] * (x.ndim - 1) + [(0, n - x.shape[-1])])


def _shift_layout(x):
    return jnp.concatenate([x[..., :3 * D_RWKV], _pad_cols(x[..., 3 * D_RWKV:], N_LORA_PAD)], axis=-1)


def _block_diag(blocks):
    nb, g, a, b = blocks.shape
    eye = jnp.eye(g, dtype=blocks.dtype)
    return jnp.einsum("mgab,gh->mgahb", blocks, eye).reshape(nb, g * a, g * b)


def _prepare(ln_in_g, ln_in_b, w_in, mu_shift, w0, w_lora_up, a0, a_lora_up, g_lora_up, k_k, k_a,
             r_k, gn_g, gn_b, s5_a_re, s5_a_im, s5_log_dt, s5_b_re, s5_b_im, s5_c_re, s5_c_im,
             s5_d, glu_w, glu_b, w_out, ln1_g, ln1_b, ffn_w1, ffn_w3, ffn_w2, ln2_g, ln2_b):
    row = lambda x: x.reshape(1, -1).astype(F32)
    w = w_in[0]
    o3 = 3 * D_RWKV
    w_rkv = w[:, :o3].astype(BF16)
    w_ul = jnp.concatenate(
        [w[:, N_SHIFT:], _pad_cols(w[:, o3:N_SHIFT], N_LORA_PAD)], axis=1).astype(BF16)
    dff = ffn_w1.shape[-1]
    nf = dff // _FFN_TF
    w13 = jnp.stack([ffn_w1[0].reshape(-1, nf, _FFN_TF), ffn_w3[0].reshape(-1, nf, _FFN_TF)],
                    axis=2).reshape(-1, 2 * dff).astype(BF16)
    wup = jnp.zeros((128, D_RWKV), F32).at[:LORA_W].set(w_lora_up[0]).astype(BF16)
    aup = jnp.zeros((128, D_RWKV), F32).at[LORA_W:LORA_W + LORA_A].set(a_lora_up[0]).astype(BF16)
    gup = jnp.zeros((256, D_RWKV), F32).at[:LORA_G].set(g_lora_up[0]).astype(BF16)
    idx = jnp.arange(256)
    tri = ((idx[:, None] >= idx[None, :]) & (idx[:, None] // CHUNK == idx[None, :] // CHUNK)).astype(BF16)
    ones = (idx[:, None] // HEAD == idx[None, :] // HEAD).astype(BF16)

    dt = jnp.exp(s5_log_dt[0].astype(F32))[:, None]
    a_re, a_im = s5_a_re[0].astype(F32), s5_a_im[0].astype(F32)
    mag = jnp.exp(a_re * dt)
    lam_re, lam_im = mag * jnp.cos(a_im * dt), mag * jnp.sin(a_im * dt)
    den = a_re * a_re + a_im * a_im
    q_re = ((lam_re - 1.0) * a_re + lam_im * a_im) / den
    q_im = (lam_im * a_re - (lam_re - 1.0) * a_im) / den
    b_re, b_im = s5_b_re[0].astype(F32), s5_b_im[0].astype(F32)
    bb_re = q_re[..., None] * b_re - q_im[..., None] * b_im
    bb_im = q_re[..., None] * b_im + q_im[..., None] * b_re
    nb = S5_GROUPS // S5_BUNDLE
    to_in = lambda x: _block_diag(jnp.swapaxes(x, 1, 2).reshape(nb, S5_BUNDLE, S5_GROUP, S5_STATE))
    to_out = lambda x: _block_diag(jnp.swapaxes(x, 1, 2).reshape(nb, S5_BUNDLE, S5_STATE, S5_GROUP))
    return {
        "ln_in_g": row(ln_in_g), "ln_in_b": row(ln_in_b), "w_rkv": w_rkv, "w_ul": w_ul,
        "mu_p": _shift_layout(row(mu_shift[0])), "w0": row(w0[0]), "wup": wup, "a0": row(a0[0]),
        "aup": aup, "gup": gup, "k_k": row(k_k[0]), "k_a": row(k_a[0]), "r_k": row(r_k[0]),
        "gn_g": row(gn_g[0]), "gn_b": row(gn_b[0]), "tri": tri, "ones": ones,
        "lam_re": lam_re.reshape(-1, 128), "lam_im": lam_im.reshape(-1, 128),
        "wb_re": to_in(bb_re).astype(BF16), "wb_im": to_in(bb_im).astype(BF16),
        "wc_re": to_out(s5_c_re[0].astype(F32)).astype(BF16),
        "wc_im": to_out(-s5_c_im[0].astype(F32)).astype(BF16),
        "s5_d": row(s5_d[0]), "glu_w": glu_w[0].astype(BF16), "glu_b": row(glu_b[0]),
        "w_out_a": w_out[0, :D_RWKV].astype(BF16), "w_out_b": w_out[0, D_RWKV:].astype(BF16),
        "ln1_g": row(ln1_g[0]), "ln1_b": row(ln1_b[0]),
        "ffn_w13": w13,
        "ffn_w2": ffn_w2[0].astype(BF16), "ln2_g": row(ln2_g[0]), "ln2_b": row(ln2_b[0]),
    }


def _run(x, shift, wkv, h_re, h_im, p):
    bsz, seq, d = x.shape
    x2d = x.reshape(bsz * seq, d)
    proj = _ln_proj(x2d, p["ln_in_g"], p["ln_in_b"], p["w_rkv"], p["w_ul"])
    proj3 = proj.reshape(bsz, seq, N_PROJ_PAD)
    o_rw, n_wkv = _rwkv(proj3, _shift_layout(shift[0]), wkv[0], p)
    o_s5, n_hre, n_him = _s5(proj3, h_re[0].reshape(bsz, S5_WIDE // 128, 128),
                             h_im[0].reshape(bsz, S5_WIDE // 128, 128), p)
    x1 = _out_ln(x2d, o_rw.reshape(bsz * seq, D_RWKV), o_s5.reshape(bsz * seq, D_S5), p)
    y = _ffn_ln(x1, p).reshape(bsz, seq, d)
    last = proj3[:, seq - 1:, :]
    lo0 = COL_LORA * N_LORA_PAD
    n_shift = jnp.concatenate([last[..., :3 * D_RWKV], last[..., lo0:lo0 + N_LORA]], axis=-1)
    return (y, n_shift[None], n_wkv[None],
            n_hre.reshape(1, bsz, S5_GROUPS, S5_STATE), n_him.reshape(1, bsz, S5_GROUPS, S5_STATE))


def kernel(x_prompt, x_sample, cache_shift, state_wkv, state_s5_re, state_s5_im, ln_in_g, ln_in_b, w_in, mu_shift, w0, w_lora_up, a0, a_lora_up, g_lora_up, k_k, k_a, r_k, gn_g, gn_b, s5_a_re, s5_a_im, s5_log_dt, s5_b_re, s5_b_im, s5_c_re, s5_c_im, s5_d, glu_w, glu_b, w_out, ln1_g, ln1_b, ffn_w1, ffn_w3, ffn_w2, ln2_g, ln2_b):
    p = _prepare(ln_in_g, ln_in_b, w_in, mu_shift, w0, w_lora_up, a0, a_lora_up, g_lora_up, k_k,
                 k_a, r_k, gn_g, gn_b, s5_a_re, s5_a_im, s5_log_dt, s5_b_re, s5_b_im, s5_c_re,
                 s5_c_im, s5_d, glu_w, glu_b, w_out, ln1_g, ln1_b, ffn_w1, ffn_w3, ffn_w2, ln2_g,
                 ln2_b)
    bp = x_prompt.shape[0]
    dt = x_prompt.dtype
    zeros = lambda *s: jnp.zeros((1, bp) + s, dt)
    yp, sh_p, wkv_p, re_p, im_p = _run(
        x_prompt, zeros(1, N_SHIFT), zeros(N_HEADS, HEAD, HEAD), zeros(S5_GROUPS, S5_STATE),
        zeros(S5_GROUPS, S5_STATE), p)
    ys, sh_s, wkv_s, re_s, im_s = _run(x_sample, cache_shift, state_wkv, state_s5_re, state_s5_im, p)
    return (yp, ys, sh_p, wkv_p, re_p, im_p, sh_s, wkv_s, re_s, im_s)
```

```python
import functools
import math

import jax
import jax.numpy as jnp
from jax import lax
from jax.experimental import pallas as pl
from jax.experimental.pallas import tpu as pltpu

F32 = jnp.float32
BF16 = jnp.bfloat16

D_MODEL = 2048
D_RWKV = 1024
D_S5 = 1024
HEAD = 64
N_HEADS = D_RWKV // HEAD
LORA_W = 64
LORA_A = 64
LORA_G = 160
N_LORA = LORA_W + LORA_A + LORA_G
N_LORA_PAD = 512
N_SHIFT = 3 * D_RWKV + N_LORA
N_SHIFT_PAD = 3 * D_RWKV + N_LORA_PAD
N_PROJ_PAD = 3 * D_RWKV + D_S5 + N_LORA_PAD
COL_U = 3
COL_LORA = 8
S5_GROUP = 16
S5_GROUPS = D_S5 // S5_GROUP
S5_STATE = 64
S5_WIDE = S5_GROUPS * S5_STATE
S5_BUNDLE = 8
CHUNK = 64
LN_EPS = 1e-5
GN_EPS = 64e-5
ALPHA = 2.0 ** 0.25
VMEM_LIMIT = 56 * 1024 * 1024
_LN_ROWS = 256


def _dot(a, b, dims=((1,), (0,))):
    return lax.dot_general(a, b, (dims, ((), ())), preferred_element_type=F32)


def _mm(a, b, dims=((1,), (0,))):
    return _dot(a.astype(BF16), b.astype(BF16), dims)


def _layer_norm(x, g, b):
    mu = jnp.mean(x, axis=-1, keepdims=True)
    xc = x - mu
    var = jnp.mean(xc * xc, axis=-1, keepdims=True)
    return xc * lax.rsqrt(var + LN_EPS) * g + b


def _row_tile(t, cap):
    for c in (1024, 512, 256, 128, 64):
        if c <= cap and t % c == 0:
            return c
    raise ValueError(f"token count {t} is not a multiple of 64")


def _const_spec(arr):
    nd = arr.ndim
    return pl.BlockSpec(arr.shape, lambda *_: (0,) * nd)


def _ln_proj_kernel(x_ref, g_ref, b_ref, w_ref, o_ref):
    for r in range(0, x_ref.shape[0], _LN_ROWS):
        rows = slice(r, r + _LN_ROWS)
        xn = _layer_norm(x_ref[rows, :], g_ref[...], b_ref[...]).astype(BF16)
        o_ref[rows, :] = _dot(xn, w_ref[...])


def _ln_proj(x2d, g, b, w_bf16):
    t, d = x2d.shape
    n = w_bf16.shape[1]
    tm = _row_tile(t, 1024)
    tn = n // 3
    return pl.pallas_call(
        _ln_proj_kernel,
        grid=(t // tm, n // tn),
        in_specs=[
            pl.BlockSpec((tm, d), lambda i, j: (i, 0)),
            pl.BlockSpec((1, d), lambda i, j: (0, 0)),
            pl.BlockSpec((1, d), lambda i, j: (0, 0)),
            pl.BlockSpec((d, tn), lambda i, j: (0, j)),
        ],
        out_specs=pl.BlockSpec((tm, tn), lambda i, j: (i, j)),
        out_shape=jax.ShapeDtypeStruct((t, n), F32),
        compiler_params=pltpu.CompilerParams(
            dimension_semantics=("parallel", "arbitrary"), vmem_limit_bytes=VMEM_LIMIT),
    )(x2d, g, b, w_bf16)


def _segsum(x, ones):
    xb = x.astype(BF16)
    return jnp.concatenate(
        [_dot(xb[:, 256 * q:256 * (q + 1)], ones) for q in range(x.shape[1] // 256)], axis=1)


def _rwkv_kernel(pr_ref, pk_ref, pv_ref, plo_ref, sh_ref, s0_ref, mu_ref, w0_ref, wup_ref,
                 a0_ref, aup_ref, gup_ref, kkw_ref, kaw_ref, rkw_ref, gng_ref, gnb_ref,
                 tri_ref, ones_ref,
                 o_ref, st_ref,
                 carry_ref, r_s, k_s, v_s, kk_s, b_s, ld_s, lc_s, g_s, y_s,
                 *, ns, tb):
    @pl.when(pl.program_id(1) == 0)
    def _():
        carry_ref[...] = sh_ref[...]
        st_ref[...] = s0_ref[...]

    row = lax.broadcasted_iota(jnp.int32, (tb, 1), 0)

    def token_shift(x_ref, c0):
        width = x_ref.shape[-1]
        outs = []
        for s in range(ns):
            x = x_ref[s]
            prev = jnp.where(row == 0, carry_ref[s, :, c0:c0 + width], pltpu.roll(x, 1, 0))
            outs.append(x + (prev - x) * mu_ref[:, c0:c0 + width])
            carry_ref[s, :, c0:c0 + width] = x[tb - 1:tb, :]
        return jnp.concatenate(outs, axis=0)

    ones = ones_ref[...]
    r = token_shift(pr_ref, 0)
    k = token_shift(pk_ref, D_RWKV)
    v = token_shift(pv_ref, 2 * D_RWKV)
    lo = token_shift(plo_ref, 3 * D_RWKV)

    lane = lax.broadcasted_iota(jnp.int32, (ns * tb, 128), 1)
    lo_wa = lo[:, 0:128]
    act_wa = jnp.where(lane < LORA_W, jnp.tanh(lo_wa), lo_wa).astype(BF16)
    act_g = jax.nn.sigmoid(lo[:, 128:384]).astype(BF16)
    w = w0_ref[...] + _dot(act_wa, wup_ref[...])
    ld = (-math.exp(-0.5)) * jax.nn.sigmoid(w)
    a = jax.nn.sigmoid(a0_ref[...] + _dot(act_wa, aup_ref[...]))
    g_s[...] = _dot(act_g, gup_ref[...])

    kk = k * kkw_ref[...]
    kk = kk * jnp.minimum(lax.rsqrt(_segsum(kk * kk, ones)), 1e12)
    k = k * (1.0 + (a - 1.0) * kaw_ref[...])

    tri = tri_ref[...]
    h1 = ld.astype(BF16)
    r1 = ld - h1.astype(F32)
    h2 = r1.astype(BF16)
    h3 = (r1 - h2.astype(F32)).astype(BF16)
    for s in range(ns):
        rows = slice(s * tb, (s + 1) * tb)
        lc_s[rows, :] = _dot(tri, h1[rows]) + (_dot(tri, h2[rows]) + _dot(tri, h3[rows]))
    ld_s[...] = ld
    r_s[...] = r
    k_s[...] = k
    v_s[...] = v
    kk_s[...] = kk
    b_s[...] = kk * a

    rr = lax.broadcasted_iota(jnp.int32, (2 * CHUNK, 2 * CHUNK), 0)
    cc = lax.broadcasted_iota(jnp.int32, (2 * CHUNK, 2 * CHUNK), 1)
    causal = cc % CHUNK < rr % CHUNK + jnp.where(rr < CHUNK, 1, 0)
    t_i = lax.broadcasted_iota(jnp.int32, (CHUNK, CHUNK), 0)
    s_i = lax.broadcasted_iota(jnp.int32, (CHUNK, CHUNK), 1)

    def same_block(size):
        return t_i // size == s_i // size

    diag_mask = jnp.where(same_block(_SOLVE_BASE), 1.0, 0.0).astype(F32)
    merge_masks = []
    size = _SOLVE_BASE
    while size < CHUNK:
        merge_masks.append(jnp.where(
            same_block(2 * size) & jnp.logical_not(same_block(size)), 1.0, 0.0).astype(F32))
        size *= 2
    lane_z = lax.broadcasted_iota(jnp.int32, (CHUNK, 4 * HEAD), 1)
    eye = jnp.where(t_i == s_i, 1.0, 0.0).astype(F32)

    def chunk_body(c, _):
        c0 = pl.multiple_of(c * CHUNK, CHUNK)
        units = []
        per_seq = []
        for s in range(ns):
            sl = pl.ds(s * tb + c0, CHUNK)
            lc = lc_s[sl, :]
            e_inc = jnp.exp(lc)
            e_exc = jnp.exp(lc - ld_s[sl, :])
            e_inv = jnp.exp(-lc)
            lc_end = lc[CHUNK - 1:CHUNK, :]
            e_hat = jnp.exp(lc_end - lc)
            kc = k_s[sl, :]
            bc = b_s[sl, :]
            per_seq.append(dict(
                sl=sl, e_end=jnp.exp(lc_end), vc=v_s[sl, :], rt=r_s[sl, :] * e_inc,
                kap=kk_s[sl, :] * e_exc, kt=kc * e_inv, bt=bc * e_inv, khat=kc * e_hat,
                bhat=bc * e_hat))
            units += [(s, h, slice(HEAD * h, HEAD * (h + 1))) for h in range(N_HEADS)]

        gms = []
        for s, h, hs in units:
            q = per_seq[s]
            lhs = jnp.concatenate([q["rt"][:, hs], q["kap"][:, hs]], axis=0)
            rhs = jnp.concatenate([q["kt"][:, hs], q["bt"][:, hs]], axis=0)
            gms.append(jnp.where(causal, _mm(lhs, rhs, ((1,), (1,))), 0.0))
        nmats = [-gm[CHUNK:, HEAD:2 * HEAD] for gm in gms]
        zeds = []
        for (s, h, hs), gm, nm in zip(units, gms, nmats):
            q = per_seq[s]
            akk_v = _mm(gm[CHUNK:, 0:HEAD], q["vc"][:, hs])
            zeds.append(jnp.concatenate(
                [nm * diag_mask, eye, akk_v, q["kap"][:, hs]], axis=1))
        steps = _SOLVE_BASE.bit_length() - 1
        for it in range(steps):
            ress = [_mm(zed[:, 0:HEAD], zed) for zed in zeds]
            if it < steps - 1:
                zeds = [jnp.where(lane_z < HEAD, res, zed + res) for zed, res in zip(zeds, ress)]
            else:
                zeds = [zed + res for zed, res in zip(zeds, ress)]
        for mask in merge_masks:
            ress = [_mm(nm * mask, zed) for nm, zed in zip(nmats, zeds)]
            ress = [_mm(zed[:, HEAD:2 * HEAD], res) for zed, res in zip(zeds, ress)]
            zeds = [zed + res for zed, res in zip(zeds, ress)]
        s_hs = [st_ref[s, h] for s, h, _ in units]
        ds = [_mm(jnp.concatenate([per_seq[s]["rt"][:, hs], zed[:, 3 * HEAD:4 * HEAD]], axis=0),
                  s_h, ((1,), (1,)))
              for (s, h, hs), zed, s_h in zip(units, zeds, s_hs)]
        vus = [jnp.concatenate([per_seq[s]["vc"][:, hs],
                                -(zed[:, 2 * HEAD:3 * HEAD] + d[CHUNK:, :])], axis=0)
               for (s, h, hs), zed, d in zip(units, zeds, ds)]
        for (s, h, hs), gm, d, vu in zip(units, gms, ds, vus):
            y_s[per_seq[s]["sl"], hs] = d[:CHUNK, :] + _mm(gm[:CHUNK, :], vu)
        for (s, h, hs), s_h, vu in zip(units, s_hs, vus):
            q = per_seq[s]
            kb_hat = jnp.concatenate([q["khat"][:, hs], q["bhat"][:, hs]], axis=0)
            st_ref[s, h] = s_h * q["e_end"][:, hs] + _mm(vu, kb_hat, ((0,), (0,)))
        return 0

    lax.fori_loop(0, tb // CHUNK, chunk_body, 0)

    y = y_s[...]
    mu = _segsum(y, ones) * (1.0 / HEAD)
    yc = y - mu
    var = _segsum(yc * yc, ones) * (1.0 / HEAD)
    yn = yc * lax.rsqrt(var + GN_EPS) * gng_ref[...] + gnb_ref[...]
    bonus = _segsum(r_s[...] * k_s[...] * rkw_ref[...], ones) * v_s[...]
    o_ref[...] = ((yn + bonus) * g_s[...]).astype(o_ref.dtype).reshape(ns, tb, D_RWKV)


_SOLVE_BASE = 8
_RWKV_SEQS = 2


def _rwkv(proj3, shift0p, wkv0, p):
    bsz, seq, _ = proj3.shape
    ns = _RWKV_SEQS
    assert bsz % ns == 0
    tb = 256 if seq % 256 == 0 else CHUNK
    nl = seq // tb

    def col(cb, width=D_RWKV):
        return pl.BlockSpec((ns, tb, width), lambda b, l, cb=cb: (b, l, cb))

    consts = [p["mu_p"], p["w0"], p["wup"], p["a0"], p["aup"], p["gup"], p["k_k"], p["k_a"],
              p["r_k"], p["gn_g"], p["gn_b"], p["tri"][:tb, :tb], p["ones"]]
    state_spec = pl.BlockSpec((ns, N_HEADS, HEAD, HEAD), lambda b, l: (b, 0, 0, 0))
    scratch = ([pltpu.VMEM((ns, 1, N_SHIFT_PAD), F32)]
               + [pltpu.VMEM((ns * tb, D_RWKV), F32)] * 9)
    return pl.pallas_call(
        functools.partial(_rwkv_kernel, ns=ns, tb=tb),
        grid=(bsz // ns, nl),
        in_specs=[col(0), col(1), col(2), col(COL_LORA, N_LORA_PAD),
                  pl.BlockSpec((ns, 1, N_SHIFT_PAD), lambda b, l: (b, 0, 0)),
                  state_spec] + [_const_spec(c) for c in consts],
        out_specs=[pl.BlockSpec((ns, tb, D_RWKV), lambda b, l: (b, l, 0)), state_spec],
        out_shape=[jax.ShapeDtypeStruct((bsz, seq, D_RWKV), BF16),
                   jax.ShapeDtypeStruct((bsz, N_HEADS, HEAD, HEAD), F32)],
        scratch_shapes=scratch,
        compiler_params=pltpu.CompilerParams(
            dimension_semantics=("parallel", "arbitrary"), vmem_limit_bytes=VMEM_LIMIT),
    )(proj3, proj3, proj3, proj3, shift0p, wkv0, *consts)


def _s5_kernel(u_ref, hre0_ref, him0_ref, lre_ref, lim_ref, wbre_ref, wbim_ref, wcre_ref,
               wcim_ref, d_ref, gluw_ref, glub_ref,
               o_ref, hre_ref, him_ref,
               *scratch, ns, tb):
    xre, xim = scratch[:ns], scratch[ns:]
    n_col = S5_WIDE // 128
    per_bundle = S5_BUNDLE * S5_STATE // 128

    @pl.when(pl.program_id(1) == 0)
    def _():
        hre_ref[...] = hre0_ref[...]
        him_ref[...] = him0_ref[...]

    kin = S5_BUNDLE * S5_GROUP
    lre = lre_ref[...]
    lim = lim_ref[...]

    def col(j):
        return pl.ds(j, tb, stride=_S5_TOK_ROWS)

    def in_map(s):
        u = u_ref[s]
        for m in range(S5_GROUPS // S5_BUNDLE):
            um = u[:, kin * m:kin * (m + 1)].astype(BF16)
            res_re = _dot(um, wbre_ref[m])
            res_im = _dot(um, wbim_ref[m])
            for c in range(per_bundle):
                xre[s][col(per_bundle * m + c), :] = res_re[:, 128 * c:128 * (c + 1)]
                xim[s][col(per_bundle * m + c), :] = res_im[:, 128 * c:128 * (c + 1)]

    def scan():
        state = [(hre_ref[s], him_ref[s]) for s in range(ns)]
        for t in range(tb):
            rows = slice(t * _S5_TOK_ROWS, t * _S5_TOK_ROWS + n_col)
            for s in range(ns):
                hre, him = state[s]
                hre, him = (lre * hre - lim * him + xre[s][rows, :],
                            lre * him + lim * hre + xim[s][rows, :])
                xre[s][rows, :] = hre
                xim[s][rows, :] = him
                state[s] = (hre, him)
        for s in range(ns):
            hre_ref[s], him_ref[s] = state[s]

    def out_map(s):
        ys = []
        for m in range(S5_GROUPS // S5_BUNDLE):
            cols = range(per_bundle * m, per_bundle * (m + 1))
            h_re = jnp.concatenate([xre[s][col(j), :] for j in cols], axis=1).astype(BF16)
            h_im = jnp.concatenate([xim[s][col(j), :] for j in cols], axis=1).astype(BF16)
            ys.append(_dot(h_re, wcre_ref[m]) + _dot(h_im, wcim_ref[m]))
        y = jnp.concatenate(ys, axis=1) + d_ref[...] * u_ref[s]
        zg = 0.5 * y * (1.0 + jnp.tanh(math.sqrt(2.0 / math.pi) * (y + 0.044715 * (y * y * y))))
        gate = jax.nn.sigmoid(_dot(zg.astype(BF16), gluw_ref[...]) + glub_ref[...])
        o_ref[s] = (zg * gate).astype(o_ref.dtype)

    for s in range(ns):
        in_map(s)
    scan()
    for s in range(ns):
        out_map(s)


_S5_SEQS = 4
_S5_TOK_ROWS = 40


def _s5(proj3, hre0, him0, p):
    bsz, seq, _ = proj3.shape
    ns = _S5_SEQS
    assert bsz % ns == 0
    tb = 128 if seq % 128 == 0 else CHUNK
    nl = seq // tb
    consts = [p["lam_re"], p["lam_im"], p["wb_re"], p["wb_im"], p["wc_re"], p["wc_im"],
              p["s5_d"], p["glu_w"], p["glu_b"]]
    state_spec = pl.BlockSpec((ns, S5_WIDE // 128, 128), lambda b, l: (b, 0, 0))
    return pl.pallas_call(
        functools.partial(_s5_kernel, ns=ns, tb=tb),
        grid=(bsz // ns, nl),
        in_specs=[pl.BlockSpec((ns, tb, D_S5), lambda b, l: (b, l, COL_U)),
                  state_spec, state_spec] + [_const_spec(c) for c in consts],
        out_specs=[pl.BlockSpec((ns, tb, D_S5), lambda b, l: (b, l, 0)), state_spec, state_spec],
        out_shape=[jax.ShapeDtypeStruct((bsz, seq, D_S5), BF16),
                   jax.ShapeDtypeStruct((bsz, S5_WIDE // 128, 128), F32),
                   jax.ShapeDtypeStruct((bsz, S5_WIDE // 128, 128), F32)],
        scratch_shapes=[pltpu.VMEM((tb * _S5_TOK_ROWS, 128), F32)] * (2 * ns),
        compiler_params=pltpu.CompilerParams(
            dimension_semantics=("parallel", "arbitrary"), vmem_limit_bytes=VMEM_LIMIT),
    )(proj3, hre0, him0, *consts)


def _out_ln_kernel(x_ref, orw_ref, os5_ref, ging_ref, binb_ref, wa_ref, wb_ref, g_ref, b_ref, o_ref):
    for r in range(0, x_ref.shape[0], _LN_ROWS):
        rows = slice(r, r + _LN_ROWS)
        xn = _layer_norm(x_ref[rows, :], ging_ref[...], binb_ref[...])
        mix = _dot(orw_ref[rows, :], wa_ref[...]) + _dot(os5_ref[rows, :], wb_ref[...])
        o_ref[rows, :] = _layer_norm(ALPHA * xn + mix, g_ref[...], b_ref[...])


def _out_ln(x2d, o_rw, o_s5, p):
    t, d = x2d.shape
    tm = _row_tile(t, 512)
    consts = [p["ln_in_g"], p["ln_in_b"], p["w_out_a"], p["w_out_b"], p["ln1_g"], p["ln1_b"]]
    return pl.pallas_call(
        _out_ln_kernel,
        grid=(t // tm,),
        in_specs=[pl.BlockSpec((tm, d), lambda i: (i, 0)),
                  pl.BlockSpec((tm, D_RWKV), lambda i: (i, 0)),
                  pl.BlockSpec((tm, D_S5), lambda i: (i, 0))] + [_const_spec(c) for c in consts],
        out_specs=pl.BlockSpec((tm, d), lambda i: (i, 0)),
        out_shape=jax.ShapeDtypeStruct((t, d), F32),
        compiler_params=pltpu.CompilerParams(
            dimension_semantics=("parallel",), vmem_limit_bytes=VMEM_LIMIT),
    )(x2d, o_rw, o_s5, *consts)


def _ffn_ln_kernel(x_ref, w1_hbm, w3_hbm, w2_hbm, g_ref, b_ref, o_ref,
                   xb_ref, w1_buf, w3_buf, w2_buf, sem, *, nf, tf):
    i = pl.program_id(0)
    n_tiles = pl.num_programs(0)

    def tile_copies(f, slot):
        cols = pl.ds(f * tf, tf)
        return (pltpu.make_async_copy(w1_hbm.at[:, cols], w1_buf.at[slot], sem.at[0, slot]),
                pltpu.make_async_copy(w3_hbm.at[:, cols], w3_buf.at[slot], sem.at[1, slot]),
                pltpu.make_async_copy(w2_hbm.at[cols, :], w2_buf.at[slot], sem.at[2, slot]))

    def start(f, slot):
        for cp in tile_copies(f, slot):
            cp.start()

    @pl.when(i == 0)
    def _():
        start(0, 0)

    xb_ref[...] = x_ref[...].astype(BF16)
    o_ref[...] = jnp.zeros_like(o_ref)

    def hidden_tile(f, _):
        slot = (i * nf + f) % 2

        @pl.when(f + 1 < nf)
        def _():
            start(f + 1, 1 - slot)

        @pl.when(jnp.logical_and(f + 1 == nf, i + 1 < n_tiles))
        def _():
            start(0, 1 - slot)

        for cp in tile_copies(f, slot):
            cp.wait()
        xb = xb_ref[...]
        h1 = _dot(xb, w1_buf[slot])
        h3 = _dot(xb, w3_buf[slot])
        act = (h1 * jax.nn.sigmoid(h1) * h3).astype(BF16)
        o_ref[...] += _dot(act, w2_buf[slot])
        return 0

    lax.fori_loop(0, nf, hidden_tile, 0)
    o_ref[...] = _layer_norm(ALPHA * x_ref[...] + o_ref[...], g_ref[...], b_ref[...])


_FFN_TF = 512


def _ffn_ln(x1, p):
    t, d = x1.shape
    dff = p["ffn_w1"].shape[1]
    tm = _row_tile(t, 512)
    tf = _FFN_TF
    nf = dff // tf
    return pl.pallas_call(
        functools.partial(_ffn_ln_kernel, nf=nf, tf=tf),
        grid=(t // tm,),
        in_specs=[pl.BlockSpec((tm, d), lambda i: (i, 0)),
                  pl.BlockSpec(memory_space=pl.ANY),
                  pl.BlockSpec(memory_space=pl.ANY),
                  pl.BlockSpec(memory_space=pl.ANY),
                  pl.BlockSpec((1, d), lambda i: (0, 0)),
                  pl.BlockSpec((1, d), lambda i: (0, 0))],
        out_specs=pl.BlockSpec((tm, d), lambda i: (i, 0)),
        out_shape=jax.ShapeDtypeStruct((t, d), F32),
        scratch_shapes=[pltpu.VMEM((tm, d), BF16),
                        pltpu.VMEM((2, d, tf), BF16),
                        pltpu.VMEM((2, d, tf), BF16),
                        pltpu.VMEM((2, tf, d), BF16),
                        pltpu.SemaphoreType.DMA((3, 2))],
        compiler_params=pltpu.CompilerParams(
            dimension_semantics=("arbitrary",), vmem_limit_bytes=VMEM_LIMIT),
    )(x1, p["ffn_w1"], p["ffn_w3"], p["ffn_w2"], p["ln2_g"], p["ln2_b"])


def _pad_cols(x, n):
    return jnp.pad(x, [(0, 0)] * (x.ndim - 1) + [(0, n - x.shape[-1])])


def _shift_layout(x):
    return jnp.concatenate([x[..., :3 * D_RWKV], _pad_cols(x[..., 3 * D_RWKV:], N_LORA_PAD)], axis=-1)


def _block_diag(blocks):
    nb, g, a, b = blocks.shape
    eye = jnp.eye(g, dtype=blocks.dtype)
    return jnp.einsum("mgab,gh->mgahb", blocks, eye).reshape(nb, g * a, g * b)


def _prepare(ln_in_g, ln_in_b, w_in, mu_shift, w0, w_lora_up, a0, a_lora_up, g_lora_up, k_k, k_a,
             r_k, gn_g, gn_b, s5_a_re, s5_a_im, s5_log_dt, s5_b_re, s5_b_im, s5_c_re, s5_c_im,
             s5_d, glu_w, glu_b, w_out, ln1_g, ln1_b, ffn_w1, ffn_w3, ffn_w2, ln2_g, ln2_b):
    row = lambda x: x.reshape(1, -1).astype(F32)
    w = w_in[0]
    o3 = 3 * D_RWKV
    w_in_p = jnp.concatenate(
        [w[:, :o3], w[:, N_SHIFT:], _pad_cols(w[:, o3:N_SHIFT], N_LORA_PAD)], axis=1).astype(BF16)
    wup = jnp.zeros((128, D_RWKV), F32).at[:LORA_W].set(w_lora_up[0]).astype(BF16)
    aup = jnp.zeros((128, D_RWKV), F32).at[LORA_W:LORA_W + LORA_A].set(a_lora_up[0]).astype(BF16)
    gup = jnp.zeros((256, D_RWKV), F32).at[:LORA_G].set(g_lora_up[0]).astype(BF16)
    idx = jnp.arange(256)
    tri = ((idx[:, None] >= idx[None, :]) & (idx[:, None] // CHUNK == idx[None, :] // CHUNK)).astype(BF16)
    ones = (idx[:, None] // HEAD == idx[None, :] // HEAD).astype(BF16)

    dt = jnp.exp(s5_log_dt[0].astype(F32))[:, None]
    a_re, a_im = s5_a_re[0].astype(F32), s5_a_im[0].astype(F32)
    mag = jnp.exp(a_re * dt)
    lam_re, lam_im = mag * jnp.cos(a_im * dt), mag * jnp.sin(a_im * dt)
    den = a_re * a_re + a_im * a_im
    q_re = ((lam_re - 1.0) * a_re + lam_im * a_im) / den
    q_im = (lam_im * a_re - (lam_re - 1.0) * a_im) / den
    b_re, b_im = s5_b_re[0].astype(F32), s5_b_im[0].astype(F32)
    bb_re = q_re[..., None] * b_re - q_im[..., None] * b_im
    bb_im = q_re[..., None] * b_im + q_im[..., None] * b_re
    nb = S5_GROUPS // S5_BUNDLE
    to_in = lambda x: _block_diag(jnp.swapaxes(x, 1, 2).reshape(nb, S5_BUNDLE, S5_GROUP, S5_STATE))
    to_out = lambda x: _block_diag(jnp.swapaxes(x, 1, 2).reshape(nb, S5_BUNDLE, S5_STATE, S5_GROUP))
    return {
        "ln_in_g": row(ln_in_g), "ln_in_b": row(ln_in_b), "w_in_p": w_in_p,
        "mu_p": _shift_layout(row(mu_shift[0])), "w0": row(w0[0]), "wup": wup, "a0": row(a0[0]),
        "aup": aup, "gup": gup, "k_k": row(k_k[0]), "k_a": row(k_a[0]), "r_k": row(r_k[0]),
        "gn_g": row(gn_g[0]), "gn_b": row(gn_b[0]), "tri": tri, "ones": ones,
        "lam_re": lam_re.reshape(-1, 128), "lam_im": lam_im.reshape(-1, 128),
        "wb_re": to_in(bb_re).astype(BF16), "wb_im": to_in(bb_im).astype(BF16),
        "wc_re": to_out(s5_c_re[0].astype(F32)).astype(BF16),
        "wc_im": to_out(-s5_c_im[0].astype(F32)).astype(BF16),
        "s5_d": row(s5_d[0]), "glu_w": glu_w[0].astype(BF16), "glu_b": row(glu_b[0]),
        "w_out_a": w_out[0, :D_RWKV].astype(BF16), "w_out_b": w_out[0, D_RWKV:].astype(BF16),
        "ln1_g": row(ln1_g[0]), "ln1_b": row(ln1_b[0]),
        "ffn_w1": ffn_w1[0].astype(BF16), "ffn_w3": ffn_w3[0].astype(BF16),
        "ffn_w2": ffn_w2[0].astype(BF16), "ln2_g": row(ln2_g[0]), "ln2_b": row(ln2_b[0]),
    }


def _run(x, shift, wkv, h_re, h_im, p):
    bsz, seq, d = x.shape
    x2d = x.reshape(bsz * seq, d)
    proj = _ln_proj(x2d, p["ln_in_g"], p["ln_in_b"], p["w_in_p"])
    proj3 = proj.reshape(bsz, seq, N_PROJ_PAD)
    o_rw, n_wkv = _rwkv(proj3, _shift_layout(shift[0]), wkv[0], p)
    o_s5, n_hre, n_him = _s5(proj3, h_re[0].reshape(bsz, S5_WIDE // 128, 128),
                             h_im[0].reshape(bsz, S5_WIDE // 128, 128), p)
    x1 = _out_ln(x2d, o_rw.reshape(bsz * seq, D_RWKV), o_s5.reshape(bsz * seq, D_S5), p)
    y = _ffn_ln(x1, p).reshape(bsz, seq, d)
    last = proj3[:, seq - 1:, :]
    lo0 = COL_LORA * N_LORA_PAD
    n_shift = jnp.concatenate([last[..., :3 * D_RWKV], last[..., lo0:lo0 + N_LORA]], axis=-1)
    return (y, n_shift[None], n_wkv[None],
            n_hre.reshape(1, bsz, S5_GROUPS, S5_STATE), n_him.reshape(1, bsz, S5_GROUPS, S5_STATE))


def kernel(x_prompt, x_sample, cache_shift, state_wkv, state_s5_re, state_s5_im, ln_in_g, ln_in_b, w_in, mu_shift, w0, w_lora_up, a0, a_lora_up, g_lora_up, k_k, k_a, r_k, gn_g, gn_b, s5_a_re, s5_a_im, s5_log_dt, s5_b_re, s5_b_im, s5_c_re, s5_c_im, s5_d, glu_w, glu_b, w_out, ln1_g, ln1_b, ffn_w1, ffn_w3, ffn_w2, ln2_g, ln2_b):
    p = _prepare(ln_in_g, ln_in_b, w_in, mu_shift, w0, w_lora_up, a0, a_lora_up, g_lora_up, k_k,
                 k_a, r_k, gn_g, gn_b, s5_a_re, s5_a_im, s5_log_dt, s5_b_re, s5_b_im, s5_c_re,
                 s5_c_im, s5_d, glu_w, glu_b, w_out, ln1_g, ln1_b, ffn_w1, ffn_w3, ffn_w2, ln2_g,
                 ln2_b)
    bp = x_prompt.shape[0]
    dt = x_prompt.dtype
    zeros = lambda *s: jnp.zeros((1, bp) + s, dt)
    yp, sh_p, wkv_p, re_p, im_p = _run(
        x_prompt, zeros(1, N_SHIFT), zeros(N_HEADS, HEAD, HEAD), zeros(S5_GROUPS, S5_STATE),
        zeros(S5_GROUPS, S5_STATE), p)
    ys, sh_s, wkv_s, re_s, im_s = _run(x_sample, cache_shift, state_wkv, state_s5_re, state_s5_im, p)
    return (yp, ys, sh_p, wkv_p, re_p, im_p, sh_s, wkv_s, re_s, im_s)
```

```python
import functools
import math

import jax
import jax.numpy as jnp
from jax import lax
from jax.experimental import pallas as pl
from jax.experimental.pallas import tpu as pltpu

F32 = jnp.float32
BF16 = jnp.bfloat16

D_MODEL = 2048
D_RWKV = 1024
D_S5 = 1024
HEAD = 64
N_HEADS = D_RWKV // HEAD
LORA_W = 64
LORA_A = 64
LORA_G = 160
N_LORA = LORA_W + LORA_A + LORA_G
N_LORA_PAD = 512
N_SHIFT = 3 * D_RWKV + N_LORA
N_SHIFT_PAD = 3 * D_RWKV + N_LORA_PAD
N_PROJ_PAD = 3 * D_RWKV + D_S5 + N_LORA_PAD
COL_U = 3
COL_LORA = 8
S5_GROUP = 16
S5_GROUPS = D_S5 // S5_GROUP
S5_STATE = 64
S5_WIDE = S5_GROUPS * S5_STATE
S5_BUNDLE = 8
CHUNK = 64
LN_EPS = 1e-5
GN_EPS = 64e-5
ALPHA = 2.0 ** 0.25
VMEM_LIMIT = 56 * 1024 * 1024
_LN_ROWS = 256
_PROJ_ROWS = 1024
_PROJ_COL_TILES = 3
_DENSE_ROWS = 512
_RWKV_TOKENS = 256
_S5_TOKENS = 128


def _dot(a, b, dims=((1,), (0,))):
    return lax.dot_general(a, b, (dims, ((), ())), preferred_element_type=F32)


def _mm(a, b, dims=((1,), (0,))):
    return _dot(a.astype(BF16), b.astype(BF16), dims)


def _layer_norm(x, g, b):
    mu = jnp.mean(x, axis=-1, keepdims=True)
    xc = x - mu
    var = jnp.mean(xc * xc, axis=-1, keepdims=True)
    return xc * lax.rsqrt(var + LN_EPS) * g + b


def _row_tile(t, cap):
    for c in (1024, 512, 256, 128, 64):
        if c <= cap and t % c == 0:
            return c
    raise ValueError(f"token count {t} is not a multiple of 64")


def _const_spec(arr):
    nd = arr.ndim
    return pl.BlockSpec(arr.shape, lambda *_: (0,) * nd)


def _ln_proj_kernel(x_ref, g_ref, b_ref, w_ref, o_ref):
    for r in range(0, x_ref.shape[0], _LN_ROWS):
        rows = slice(r, r + _LN_ROWS)
        xn = _layer_norm(x_ref[rows, :], g_ref[...], b_ref[...]).astype(BF16)
        o_ref[rows, :] = _dot(xn, w_ref[...])


def _ln_proj(x2d, g, b, w_bf16):
    t, d = x2d.shape
    n = w_bf16.shape[1]
    tm = _row_tile(t, _PROJ_ROWS)
    tn = n // _PROJ_COL_TILES
    return pl.pallas_call(
        _ln_proj_kernel,
        grid=(t // tm, n // tn),
        in_specs=[
            pl.BlockSpec((tm, d), lambda i, j: (i, 0)),
            pl.BlockSpec((1, d), lambda i, j: (0, 0)),
            pl.BlockSpec((1, d), lambda i, j: (0, 0)),
            pl.BlockSpec((d, tn), lambda i, j: (0, j)),
        ],
        out_specs=pl.BlockSpec((tm, tn), lambda i, j: (i, j)),
        out_shape=jax.ShapeDtypeStruct((t, n), F32),
        compiler_params=pltpu.CompilerParams(
            dimension_semantics=("parallel", "arbitrary"), vmem_limit_bytes=VMEM_LIMIT),
    )(x2d, g, b, w_bf16)


def _segsum(x, ones):
    xb = x.astype(BF16)
    return jnp.concatenate(
        [_dot(xb[:, 256 * q:256 * (q + 1)], ones) for q in range(x.shape[1] // 256)], axis=1)


def _rwkv_kernel(pr_ref, pk_ref, pv_ref, plo_ref, sh_ref, s0_ref, mu_ref, w0_ref, wup_ref,
                 a0_ref, aup_ref, gup_ref, kkw_ref, kaw_ref, rkw_ref, gng_ref, gnb_ref,
                 tri_ref, ones_ref,
                 o_ref, st_ref,
                 carry_ref, r_s, k_s, v_s, kk_s, b_s, ld_s, lc_s, g_s, y_s,
                 *, ns, tb):
    @pl.when(pl.program_id(1) == 0)
    def _():
        carry_ref[...] = sh_ref[...]
        st_ref[...] = s0_ref[...]

    row = lax.broadcasted_iota(jnp.int32, (tb, 1), 0)

    def token_shift(x_ref, c0):
        width = x_ref.shape[-1]
        outs = []
        for s in range(ns):
            x = x_ref[s]
            prev = jnp.where(row == 0, carry_ref[s, :, c0:c0 + width], pltpu.roll(x, 1, 0))
            outs.append(x + (prev - x) * mu_ref[:, c0:c0 + width])
            carry_ref[s, :, c0:c0 + width] = x[tb - 1:tb, :]
        return jnp.concatenate(outs, axis=0)

    ones = ones_ref[...]
    r = token_shift(pr_ref, 0)
    k = token_shift(pk_ref, D_RWKV)
    v = token_shift(pv_ref, 2 * D_RWKV)
    lo = token_shift(plo_ref, 3 * D_RWKV)

    lane = lax.broadcasted_iota(jnp.int32, (ns * tb, 128), 1)
    lo_wa = lo[:, 0:128]
    act_wa = jnp.where(lane < LORA_W, jnp.tanh(lo_wa), lo_wa).astype(BF16)
    act_g = jax.nn.sigmoid(lo[:, 128:384]).astype(BF16)
    w = w0_ref[...] + _dot(act_wa, wup_ref[...])
    ld = (-math.exp(-0.5)) * jax.nn.sigmoid(w)
    a = jax.nn.sigmoid(a0_ref[...] + _dot(act_wa, aup_ref[...]))
    g_s[...] = _dot(act_g, gup_ref[...])

    kk = k * kkw_ref[...]
    kk = kk * jnp.minimum(lax.rsqrt(_segsum(kk * kk, ones)), 1e12)
    k = k * (1.0 + (a - 1.0) * kaw_ref[...])

    tri = tri_ref[...]
    h1 = ld.astype(BF16)
    r1 = ld - h1.astype(F32)
    h2 = r1.astype(BF16)
    h3 = (r1 - h2.astype(F32)).astype(BF16)
    for s in range(ns):
        rows = slice(s * tb, (s + 1) * tb)
        lc_s[rows, :] = _dot(tri, h1[rows]) + (_dot(tri, h2[rows]) + _dot(tri, h3[rows]))
    ld_s[...] = ld
    r_s[...] = r
    k_s[...] = k
    v_s[...] = v
    kk_s[...] = kk
    b_s[...] = kk * a

    rr = lax.broadcasted_iota(jnp.int32, (2 * CHUNK, 2 * CHUNK), 0)
    cc = lax.broadcasted_iota(jnp.int32, (2 * CHUNK, 2 * CHUNK), 1)
    causal = cc % CHUNK < rr % CHUNK + jnp.where(rr < CHUNK, 1, 0)
    t_i = lax.broadcasted_iota(jnp.int32, (CHUNK, CHUNK), 0)
    s_i = lax.broadcasted_iota(jnp.int32, (CHUNK, CHUNK), 1)

    def same_block(size):
        return t_i // size == s_i // size

    diag_mask = jnp.where(same_block(_SOLVE_BASE), 1.0, 0.0).astype(F32)
    merge_masks = []
    size = _SOLVE_BASE
    while size < CHUNK:
        merge_masks.append(jnp.where(
            same_block(2 * size) & jnp.logical_not(same_block(size)), 1.0, 0.0).astype(F32))
        size *= 2
    lane_z = lax.broadcasted_iota(jnp.int32, (CHUNK, 4 * HEAD), 1)
    eye = jnp.where(t_i == s_i, 1.0, 0.0).astype(F32)

    def chunk_body(c, _):
        c0 = pl.multiple_of(c * CHUNK, CHUNK)
        units = []
        per_seq = []
        for s in range(ns):
            sl = pl.ds(s * tb + c0, CHUNK)
            lc = lc_s[sl, :]
            e_inc = jnp.exp(lc)
            e_exc = jnp.exp(lc - ld_s[sl, :])
            e_inv = jnp.exp(-lc)
            lc_end = lc[CHUNK - 1:CHUNK, :]
            e_hat = jnp.exp(lc_end - lc)
            kc = k_s[sl, :]
            bc = b_s[sl, :]
            per_seq.append(dict(
                sl=sl, e_end=jnp.exp(lc_end), vc=v_s[sl, :], rt=r_s[sl, :] * e_inc,
                kap=kk_s[sl, :] * e_exc, kt=kc * e_inv, bt=bc * e_inv, khat=kc * e_hat,
                bhat=bc * e_hat))
            units += [(s, h, slice(HEAD * h, HEAD * (h + 1))) for h in range(N_HEADS)]

        gms = []
        for s, h, hs in units:
            q = per_seq[s]
            lhs = jnp.concatenate([q["rt"][:, hs], q["kap"][:, hs]], axis=0)
            rhs = jnp.concatenate([q["kt"][:, hs], q["bt"][:, hs]], axis=0)
            gms.append(jnp.where(causal, _mm(lhs, rhs, ((1,), (1,))), 0.0))
        nmats = [-gm[CHUNK:, HEAD:2 * HEAD] for gm in gms]
        zeds = []
        for (s, h, hs), gm, nm in zip(units, gms, nmats):
            q = per_seq[s]
            akk_v = _mm(gm[CHUNK:, 0:HEAD], q["vc"][:, hs])
            zeds.append(jnp.concatenate(
                [nm * diag_mask, eye, akk_v, q["kap"][:, hs]], axis=1))
        steps = _SOLVE_BASE.bit_length() - 1
        for it in range(steps):
            ress = [_mm(zed[:, 0:HEAD], zed) for zed in zeds]
            if it < steps - 1:
                zeds = [jnp.where(lane_z < HEAD, res, zed + res) for zed, res in zip(zeds, ress)]
            else:
                zeds = [zed + res for zed, res in zip(zeds, ress)]
        for mask in merge_masks:
            ress = [_mm(nm * mask, zed) for nm, zed in zip(nmats, zeds)]
            ress = [_mm(zed[:, HEAD:2 * HEAD], res) for zed, res in zip(zeds, ress)]
            zeds = [zed + res for zed, res in zip(zeds, ress)]
        s_hs = [st_ref[s, h] for s, h, _ in units]
        ds = [_mm(jnp.concatenate([per_seq[s]["rt"][:, hs], zed[:, 3 * HEAD:4 * HEAD]], axis=0),
                  s_h, ((1,), (1,)))
              for (s, h, hs), zed, s_h in zip(units, zeds, s_hs)]
        vus = [jnp.concatenate([per_seq[s]["vc"][:, hs],
                                -(zed[:, 2 * HEAD:3 * HEAD] + d[CHUNK:, :])], axis=0)
               for (s, h, hs), zed, d in zip(units, zeds, ds)]
        for (s, h, hs), gm, d, vu in zip(units, gms, ds, vus):
            y_s[per_seq[s]["sl"], hs] = d[:CHUNK, :] + _mm(gm[:CHUNK, :], vu)
        for (s, h, hs), s_h, vu in zip(units, s_hs, vus):
            q = per_seq[s]
            kb_hat = jnp.concatenate([q["khat"][:, hs], q["bhat"][:, hs]], axis=0)
            st_ref[s, h] = s_h * q["e_end"][:, hs] + _mm(vu, kb_hat, ((0,), (0,)))
        return 0

    lax.fori_loop(0, tb // CHUNK, chunk_body, 0)

    y = y_s[...]
    mu = _segsum(y, ones) * (1.0 / HEAD)
    yc = y - mu
    var = _segsum(yc * yc, ones) * (1.0 / HEAD)
    yn = yc * lax.rsqrt(var + GN_EPS) * gng_ref[...] + gnb_ref[...]
    bonus = _segsum(r_s[...] * k_s[...] * rkw_ref[...], ones) * v_s[...]
    o_ref[...] = ((yn + bonus) * g_s[...]).astype(o_ref.dtype).reshape(ns, tb, D_RWKV)


_SOLVE_BASE = 8
_RWKV_SEQS = 2


def _rwkv(proj3, shift0p, wkv0, p):
    bsz, seq, _ = proj3.shape
    ns = _RWKV_SEQS
    assert bsz % ns == 0
    tb = _RWKV_TOKENS if seq % _RWKV_TOKENS == 0 else CHUNK
    nl = seq // tb

    def col(cb, width=D_RWKV):
        return pl.BlockSpec((ns, tb, width), lambda b, l, cb=cb: (b, l, cb))

    consts = [p["mu_p"], p["w0"], p["wup"], p["a0"], p["aup"], p["gup"], p["k_k"], p["k_a"],
              p["r_k"], p["gn_g"], p["gn_b"], p["tri"][:tb, :tb], p["ones"]]
    state_spec = pl.BlockSpec((ns, N_HEADS, HEAD, HEAD), lambda b, l: (b, 0, 0, 0))
    scratch = ([pltpu.VMEM((ns, 1, N_SHIFT_PAD), F32)]
               + [pltpu.VMEM((ns * tb, D_RWKV), F32)] * 9)
    return pl.pallas_call(
        functools.partial(_rwkv_kernel, ns=ns, tb=tb),
        grid=(bsz // ns, nl),
        in_specs=[col(0), col(1), col(2), col(COL_LORA, N_LORA_PAD),
                  pl.BlockSpec((ns, 1, N_SHIFT_PAD), lambda b, l: (b, 0, 0)),
                  state_spec] + [_const_spec(c) for c in consts],
        out_specs=[pl.BlockSpec((ns, tb, D_RWKV), lambda b, l: (b, l, 0)), state_spec],
        out_shape=[jax.ShapeDtypeStruct((bsz, seq, D_RWKV), BF16),
                   jax.ShapeDtypeStruct((bsz, N_HEADS, HEAD, HEAD), F32)],
        scratch_shapes=scratch,
        compiler_params=pltpu.CompilerParams(
            dimension_semantics=("parallel", "arbitrary"), vmem_limit_bytes=VMEM_LIMIT),
    )(proj3, proj3, proj3, proj3, shift0p, wkv0, *consts)


def _s5_kernel(u_ref, hre0_ref, him0_ref, lre_ref, lim_ref, wbre_ref, wbim_ref, wcre_ref,
               wcim_ref, d_ref, gluw_ref, glub_ref,
               o_ref, hre_ref, him_ref,
               *scratch, ns, tb):
    xre, xim = scratch[:ns], scratch[ns:]
    n_col = S5_WIDE // 128
    per_bundle = S5_BUNDLE * S5_STATE // 128

    @pl.when(pl.program_id(1) == 0)
    def _():
        hre_ref[...] = hre0_ref[...]
        him_ref[...] = him0_ref[...]

    kin = S5_BUNDLE * S5_GROUP
    lre = lre_ref[...]
    lim = lim_ref[...]

    def col(j):
        return pl.ds(j, tb, stride=_S5_TOK_ROWS)

    def in_map(s):
        u = u_ref[s]
        for m in range(S5_GROUPS // S5_BUNDLE):
            um = u[:, kin * m:kin * (m + 1)].astype(BF16)
            res_re = _dot(um, wbre_ref[m])
            res_im = _dot(um, wbim_ref[m])
            for c in range(per_bundle):
                xre[s][col(per_bundle * m + c), :] = res_re[:, 128 * c:128 * (c + 1)]
                xim[s][col(per_bundle * m + c), :] = res_im[:, 128 * c:128 * (c + 1)]

    def scan():
        state = [(hre_ref[s], him_ref[s]) for s in range(ns)]
        for t in range(tb):
            rows = slice(t * _S5_TOK_ROWS, t * _S5_TOK_ROWS + n_col)
            for s in range(ns):
                hre, him = state[s]
                hre, him = (lre * hre - lim * him + xre[s][rows, :],
                            lre * him + lim * hre + xim[s][rows, :])
                xre[s][rows, :] = hre
                xim[s][rows, :] = him
                state[s] = (hre, him)
        for s in range(ns):
            hre_ref[s], him_ref[s] = state[s]

    def out_map(s):
        ys = []
        for m in range(S5_GROUPS // S5_BUNDLE):
            cols = range(per_bundle * m, per_bundle * (m + 1))
            h_re = jnp.concatenate([xre[s][col(j), :] for j in cols], axis=1).astype(BF16)
            h_im = jnp.concatenate([xim[s][col(j), :] for j in cols], axis=1).astype(BF16)
            ys.append(_dot(h_re, wcre_ref[m]) + _dot(h_im, wcim_ref[m]))
        y = jnp.concatenate(ys, axis=1) + d_ref[...] * u_ref[s]
        zg = 0.5 * y * (1.0 + jnp.tanh(math.sqrt(2.0 / math.pi) * (y + 0.044715 * (y * y * y))))
        gate = jax.nn.sigmoid(_dot(zg.astype(BF16), gluw_ref[...]) + glub_ref[...])
        o_ref[s] = (zg * gate).astype(o_ref.dtype)

    for s in range(ns):
        in_map(s)
    scan()
    for s in range(ns):
        out_map(s)


_S5_SEQS = 4
_S5_TOK_ROWS = 40


def _s5(proj3, hre0, him0, p):
    bsz, seq, _ = proj3.shape
    ns = _S5_SEQS
    assert bsz % ns == 0
    tb = _S5_TOKENS if seq % _S5_TOKENS == 0 else CHUNK
    nl = seq // tb
    consts = [p["lam_re"], p["lam_im"], p["wb_re"], p["wb_im"], p["wc_re"], p["wc_im"],
              p["s5_d"], p["glu_w"], p["glu_b"]]
    state_spec = pl.BlockSpec((ns, S5_WIDE // 128, 128), lambda b, l: (b, 0, 0))
    return pl.pallas_call(
        functools.partial(_s5_kernel, ns=ns, tb=tb),
        grid=(bsz // ns, nl),
        in_specs=[pl.BlockSpec((ns, tb, D_S5), lambda b, l: (b, l, COL_U)),
                  state_spec, state_spec] + [_const_spec(c) for c in consts],
        out_specs=[pl.BlockSpec((ns, tb, D_S5), lambda b, l: (b, l, 0)), state_spec, state_spec],
        out_shape=[jax.ShapeDtypeStruct((bsz, seq, D_S5), BF16),
                   jax.ShapeDtypeStruct((bsz, S5_WIDE // 128, 128), F32),
                   jax.ShapeDtypeStruct((bsz, S5_WIDE // 128, 128), F32)],
        scratch_shapes=[pltpu.VMEM((tb * _S5_TOK_ROWS, 128), F32)] * (2 * ns),
        compiler_params=pltpu.CompilerParams(
            dimension_semantics=("parallel", "arbitrary"), vmem_limit_bytes=VMEM_LIMIT),
    )(proj3, hre0, him0, *consts)


def _out_ln_kernel(x_ref, orw_ref, os5_ref, ging_ref, binb_ref, wa_ref, wb_ref, g_ref, b_ref, o_ref):
    for r in range(0, x_ref.shape[0], _LN_ROWS):
        rows = slice(r, r + _LN_ROWS)
        xn = _layer_norm(x_ref[rows, :], ging_ref[...], binb_ref[...])
        mix = _dot(orw_ref[rows, :], wa_ref[...]) + _dot(os5_ref[rows, :], wb_ref[...])
        o_ref[rows, :] = _layer_norm(ALPHA * xn + mix, g_ref[...], b_ref[...])


def _out_ln(x2d, o_rw, o_s5, p):
    t, d = x2d.shape
    tm = _row_tile(t, _DENSE_ROWS)
    consts = [p["ln_in_g"], p["ln_in_b"], p["w_out_a"], p["w_out_b"], p["ln1_g"], p["ln1_b"]]
    return pl.pallas_call(
        _out_ln_kernel,
        grid=(t // tm,),
        in_specs=[pl.BlockSpec((tm, d), lambda i: (i, 0)),
                  pl.BlockSpec((tm, D_RWKV), lambda i: (i, 0)),
                  pl.BlockSpec((tm, D_S5), lambda i: (i, 0))] + [_const_spec(c) for c in consts],
        out_specs=pl.BlockSpec((tm, d), lambda i: (i, 0)),
        out_shape=jax.ShapeDtypeStruct((t, d), F32),
        compiler_params=pltpu.CompilerParams(
            dimension_semantics=("parallel",), vmem_limit_bytes=VMEM_LIMIT),
    )(x2d, o_rw, o_s5, *consts)


def _ffn_ln_kernel(x_ref, w1_ref, w3_ref, w2_ref, g_ref, b_ref, o_ref, xb_ref, *, nf):
    f = pl.program_id(1)

    @pl.when(f == 0)
    def _():
        xb_ref[...] = x_ref[...].astype(BF16)
        o_ref[...] = jnp.zeros_like(o_ref)

    xb = xb_ref[...]
    h1 = _dot(xb, w1_ref[...])
    h3 = _dot(xb, w3_ref[...])
    act = (h1 * jax.nn.sigmoid(h1) * h3).astype(BF16)
    o_ref[...] += _dot(act, w2_ref[...])

    @pl.when(f == nf - 1)
    def _():
        o_ref[...] = _layer_norm(ALPHA * x_ref[...] + o_ref[...], g_ref[...], b_ref[...])


_FFN_TF = 512


def _ffn_ln(x1, p):
    t, d = x1.shape
    dff = p["ffn_w1"].shape[1]
    tm = _row_tile(t, _DENSE_ROWS)
    tf = _FFN_TF
    nf = dff // tf
    return pl.pallas_call(
        functools.partial(_ffn_ln_kernel, nf=nf),
        grid=(t // tm, nf),
        in_specs=[pl.BlockSpec((tm, d), lambda i, f: (i, 0)),
                  pl.BlockSpec((d, tf), lambda i, f: (0, f)),
                  pl.BlockSpec((d, tf), lambda i, f: (0, f)),
                  pl.BlockSpec((tf, d), lambda i, f: (f, 0)),
                  pl.BlockSpec((1, d), lambda i, f: (0, 0)),
                  pl.BlockSpec((1, d), lambda i, f: (0, 0))],
        out_specs=pl.BlockSpec((tm, d), lambda i, f: (i, 0)),
        out_shape=jax.ShapeDtypeStruct((t, d), F32),
        scratch_shapes=[pltpu.VMEM((tm, d), BF16)],
        compiler_params=pltpu.CompilerParams(
            dimension_semantics=("parallel", "arbitrary"), vmem_limit_bytes=VMEM_LIMIT),
    )(x1, p["ffn_w1"], p["ffn_w3"], p["ffn_w2"], p["ln2_g"], p["ln2_b"])


def _pad_cols(x, n):
    return jnp.pad(x, [(0, 0)] * (x.ndim - 1) + [(0, n - x.shape[-1])])


def _shift_layout(x):
    return jnp.concatenate([x[..., :3 * D_RWKV], _pad_cols(x[..., 3 * D_RWKV:], N_LORA_PAD)], axis=-1)


def _block_diag(blocks):
    nb, g, a, b = blocks.shape
    eye = jnp.eye(g, dtype=blocks.dtype)
    return jnp.einsum("mgab,gh->mgahb", blocks, eye).reshape(nb, g * a, g * b)


def _prepare(ln_in_g, ln_in_b, w_in, mu_shift, w0, w_lora_up, a0, a_lora_up, g_lora_up, k_k, k_a,
             r_k, gn_g, gn_b, s5_a_re, s5_a_im, s5_log_dt, s5_b_re, s5_b_im, s5_c_re, s5_c_im,
             s5_d, glu_w, glu_b, w_out, ln1_g, ln1_b, ffn_w1, ffn_w3, ffn_w2, ln2_g, ln2_b):
    row = lambda x: x.reshape(1, -1).astype(F32)
    w = w_in[0]
    o3 = 3 * D_RWKV
    w_in_p = jnp.concatenate(
        [w[:, :o3], w[:, N_SHIFT:], _pad_cols(w[:, o3:N_SHIFT], N_LORA_PAD)], axis=1).astype(BF16)
    wup = jnp.zeros((128, D_RWKV), F32).at[:LORA_W].set(w_lora_up[0]).astype(BF16)
    aup = jnp.zeros((128, D_RWKV), F32).at[LORA_W:LORA_W + LORA_A].set(a_lora_up[0]).astype(BF16)
    gup = jnp.zeros((256, D_RWKV), F32).at[:LORA_G].set(g_lora_up[0]).astype(BF16)
    idx = jnp.arange(256)
    tri = ((idx[:, None] >= idx[None, :]) & (idx[:, None] // CHUNK == idx[None, :] // CHUNK)).astype(BF16)
    ones = (idx[:, None] // HEAD == idx[None, :] // HEAD).astype(BF16)

    dt = jnp.exp(s5_log_dt[0].astype(F32))[:, None]
    a_re, a_im = s5_a_re[0].astype(F32), s5_a_im[0].astype(F32)
    mag = jnp.exp(a_re * dt)
    lam_re, lam_im = mag * jnp.cos(a_im * dt), mag * jnp.sin(a_im * dt)
    den = a_re * a_re + a_im * a_im
    q_re = ((lam_re - 1.0) * a_re + lam_im * a_im) / den
    q_im = (lam_im * a_re - (lam_re - 1.0) * a_im) / den
    b_re, b_im = s5_b_re[0].astype(F32), s5_b_im[0].astype(F32)
    bb_re = q_re[..., None] * b_re - q_im[..., None] * b_im
    bb_im = q_re[..., None] * b_im + q_im[..., None] * b_re
    nb = S5_GROUPS // S5_BUNDLE
    to_in = lambda x: _block_diag(jnp.swapaxes(x, 1, 2).reshape(nb, S5_BUNDLE, S5_GROUP, S5_STATE))
    to_out = lambda x: _block_diag(jnp.swapaxes(x, 1, 2).reshape(nb, S5_BUNDLE, S5_STATE, S5_GROUP))
    return {
        "ln_in_g": row(ln_in_g), "ln_in_b": row(ln_in_b), "w_in_p": w_in_p,
        "mu_p": _shift_layout(row(mu_shift[0])), "w0": row(w0[0]), "wup": wup, "a0": row(a0[0]),
        "aup": aup, "gup": gup, "k_k": row(k_k[0]), "k_a": row(k_a[0]), "r_k": row(r_k[0]),
        "gn_g": row(gn_g[0]), "gn_b": row(gn_b[0]), "tri": tri, "ones": ones,
        "lam_re": lam_re.reshape(-1, 128), "lam_im": lam_im.reshape(-1, 128),
        "wb_re": to_in(bb_re).astype(BF16), "wb_im": to_in(bb_im).astype(BF16),
        "wc_re": to_out(s5_c_re[0].astype(F32)).astype(BF16),
        "wc_im": to_out(-s5_c_im[0].astype(F32)).astype(BF16),
        "s5_d": row(s5_d[0]), "glu_w": glu_w[0].astype(BF16), "glu_b": row(glu_b[0]),
        "w_out_a": w_out[0, :D_RWKV].astype(BF16), "w_out_b": w_out[0, D_RWKV:].astype(BF16),
        "ln1_g": row(ln1_g[0]), "ln1_b": row(ln1_b[0]),
        "ffn_w1": ffn_w1[0].astype(BF16), "ffn_w3": ffn_w3[0].astype(BF16),
        "ffn_w2": ffn_w2[0].astype(BF16), "ln2_g": row(ln2_g[0]), "ln2_b": row(ln2_b[0]),
    }


def _run(x, shift, wkv, h_re, h_im, p):
    bsz, seq, d = x.shape
    x2d = x.reshape(bsz * seq, d)
    proj = _ln_proj(x2d, p["ln_in_g"], p["ln_in_b"], p["w_in_p"])
    proj3 = proj.reshape(bsz, seq, N_PROJ_PAD)
    o_rw, n_wkv = _rwkv(proj3, _shift_layout(shift[0]), wkv[0], p)
    o_s5, n_hre, n_him = _s5(proj3, h_re[0].reshape(bsz, S5_WIDE // 128, 128),
                             h_im[0].reshape(bsz, S5_WIDE // 128, 128), p)
    x1 = _out_ln(x2d, o_rw.reshape(bsz * seq, D_RWKV), o_s5.reshape(bsz * seq, D_S5), p)
    y = _ffn_ln(x1, p).reshape(bsz, seq, d)
    last = proj3[:, seq - 1:, :]
    lo0 = COL_LORA * N_LORA_PAD
    n_shift = jnp.concatenate([last[..., :3 * D_RWKV], last[..., lo0:lo0 + N_LORA]], axis=-1)
    return (y, n_shift[None], n_wkv[None],
            n_hre.reshape(1, bsz, S5_GROUPS, S5_STATE), n_him.reshape(1, bsz, S5_GROUPS, S5_STATE))


def kernel(x_prompt, x_sample, cache_shift, state_wkv, state_s5_re, state_s5_im, ln_in_g, ln_in_b, w_in, mu_shift, w0, w_lora_up, a0, a_lora_up, g_lora_up, k_k, k_a, r_k, gn_g, gn_b, s5_a_re, s5_a_im, s5_log_dt, s5_b_re, s5_b_im, s5_c_re, s5_c_im, s5_d, glu_w, glu_b, w_out, ln1_g, ln1_b, ffn_w1, ffn_w3, ffn_w2, ln2_g, ln2_b):
    p = _prepare(ln_in_g, ln_in_b, w_in, mu_shift, w0, w_lora_up, a0, a_lora_up, g_lora_up, k_k,
                 k_a, r_k, gn_g, gn_b, s5_a_re, s5_a_im, s5_log_dt, s5_b_re, s5_b_im, s5_c_re,
                 s5_c_im, s5_d, glu_w, glu_b, w_out, ln1_g, ln1_b, ffn_w1, ffn_w3, ffn_w2, ln2_g,
                 ln2_b)
    bp = x_prompt.shape[0]
    dt = x_prompt.dtype
    zeros = lambda *s: jnp.zeros((1, bp) + s, dt)
    yp, sh_p, wkv_p, re_p, im_p = _run(
        x_prompt, zeros(1, N_SHIFT), zeros(N_HEADS, HEAD, HEAD), zeros(S5_GROUPS, S5_STATE),
        zeros(S5_GROUPS, S5_STATE), p)
    ys, sh_s, wkv_s, re_s, im_s = _run(x_sample, cache_shift, state_wkv, state_s5_re, state_s5_im, p)
    return (yp, ys, sh_p, wkv_p, re_p, im_p, sh_s, wkv_s, re_s, im_s)
```

```python
import functools
import math

import jax
import jax.numpy as jnp
from jax import lax
from jax.experimental import pallas as pl
from jax.experimental.pallas import tpu as pltpu

F32 = jnp.float32
BF16 = jnp.bfloat16

D_MODEL = 2048
D_RWKV = 1024
D_S5 = 1024
HEAD = 64
N_HEADS = D_RWKV // HEAD
LORA_W = 64
LORA_A = 64
LORA_G = 160
N_LORA = LORA_W + LORA_A + LORA_G
N_LORA_PAD = 512
N_SHIFT = 3 * D_RWKV + N_LORA
N_SHIFT_PAD = 3 * D_RWKV + N_LORA_PAD
N_PROJ_PAD = 3 * D_RWKV + D_S5 + N_LORA_PAD
COL_U = 3
COL_LORA = 8
S5_GROUP = 16
S5_GROUPS = D_S5 // S5_GROUP
S5_STATE = 64
S5_WIDE = S5_GROUPS * S5_STATE
S5_BUNDLE = 8
CHUNK = 64
LN_EPS = 1e-5
GN_EPS = 64e-5
ALPHA = 2.0 ** 0.25
VMEM_LIMIT = 56 * 1024 * 1024
_LN_ROWS = 256
_PROJ_ROWS = 1024
_PROJ_COL_TILES = 3
_DENSE_ROWS = 512
_RWKV_TOKENS = 256
_S5_TOKENS = 128


def _dot(a, b, dims=((1,), (0,))):
    return lax.dot_general(a, b, (dims, ((), ())), preferred_element_type=F32)


def _mm(a, b, dims=((1,), (0,))):
    return _dot(a.astype(BF16), b.astype(BF16), dims)


def _layer_norm(x, g, b):
    mu = jnp.mean(x, axis=-1, keepdims=True)
    xc = x - mu
    var = jnp.mean(xc * xc, axis=-1, keepdims=True)
    return xc * lax.rsqrt(var + LN_EPS) * g + b


def _row_tile(t, cap):
    for c in (1024, 512, 256, 128, 64):
        if c <= cap and t % c == 0:
            return c
    raise ValueError(f"token count {t} is not a multiple of 64")


def _const_spec(arr):
    nd = arr.ndim
    return pl.BlockSpec(arr.shape, lambda *_: (0,) * nd)


def _ln_proj_kernel(x_ref, g_ref, b_ref, w_ref, *rest, tiles_per_seq):
    if tiles_per_seq:
        mu_ref, sh0_ref, o_ref, last_ref, carry_ref = rest
        j = pl.program_id(1)

        @pl.when(pl.program_id(0) % tiles_per_seq == 0)
        def _():
            carry_ref[j] = sh0_ref[...]

        row0 = lax.broadcasted_iota(jnp.int32, (_LN_ROWS, 1), 0) == 0
        before = carry_ref[j]
    else:
        (o_ref,) = rest
    for r in range(0, x_ref.shape[0], _LN_ROWS):
        rows = slice(r, r + _LN_ROWS)
        xn = _layer_norm(x_ref[rows, :], g_ref[...], b_ref[...]).astype(BF16)
        y = _dot(xn, w_ref[...])
        if tiles_per_seq:
            prev = jnp.where(row0, before, pltpu.roll(y, 1, 0))
            before = y[_LN_ROWS - 1:, :]
            y = y + (prev - y) * mu_ref[...]
        o_ref[rows, :] = y
    if tiles_per_seq:
        carry_ref[j] = before
        last_ref[...] = before


def _ln_proj(x2d, g, b, w_bf16, shift=None):
    t, d = x2d.shape
    n = w_bf16.shape[1]
    tm = _row_tile(t, _PROJ_ROWS)
    tn = n // _PROJ_COL_TILES
    x_spec = pl.BlockSpec((tm, d), lambda i, j: (i, 0))
    row_spec = pl.BlockSpec((1, d), lambda i, j: (0, 0))
    w_spec = pl.BlockSpec((d, tn), lambda i, j: (0, j))
    o_spec = pl.BlockSpec((tm, tn), lambda i, j: (i, j))
    if shift is None:
        proj = pl.pallas_call(
            functools.partial(_ln_proj_kernel, tiles_per_seq=0),
            grid=(t // tm, n // tn),
            in_specs=[x_spec, row_spec, row_spec, w_spec],
            out_specs=o_spec,
            out_shape=jax.ShapeDtypeStruct((t, n), F32),
            compiler_params=pltpu.CompilerParams(
                dimension_semantics=("parallel", "arbitrary"), vmem_limit_bytes=VMEM_LIMIT),
        )(x2d, g, b, w_bf16)
        return proj, None
    mu, shift0, seq = shift
    tiles_per_seq = seq // tm
    bsz = t // seq
    proj, tile_last = pl.pallas_call(
        functools.partial(_ln_proj_kernel, tiles_per_seq=tiles_per_seq),
        grid=(t // tm, n // tn),
        in_specs=[x_spec, row_spec, row_spec, w_spec,
                  pl.BlockSpec((1, tn), lambda i, j: (0, j)),
                  pl.BlockSpec((None, 1, tn), lambda i, j: (i // tiles_per_seq, 0, j))],
        out_specs=[o_spec, pl.BlockSpec((None, 1, tn), lambda i, j: (i, 0, j))],
        out_shape=[jax.ShapeDtypeStruct((t, n), F32),
                   jax.ShapeDtypeStruct((t // tm, 1, n), F32)],
        scratch_shapes=[pltpu.VMEM((n // tn, 1, tn), F32)],
        compiler_params=pltpu.CompilerParams(
            dimension_semantics=("arbitrary", "arbitrary"), vmem_limit_bytes=VMEM_LIMIT),
    )(x2d, g, b, w_bf16, mu, shift0)
    return proj, tile_last[tiles_per_seq - 1::tiles_per_seq]


def _segsum(x, ones):
    xb = x.astype(BF16)
    return jnp.concatenate(
        [_dot(xb[:, 256 * q:256 * (q + 1)], ones) for q in range(x.shape[1] // 256)], axis=1)


def _rwkv_kernel(pr_ref, pk_ref, pv_ref, plo_ref, sh_ref, s0_ref, mu_ref, w0_ref, wup_ref,
                 a0_ref, aup_ref, gup_ref, kkw_ref, kaw_ref, rkw_ref, gng_ref, gnb_ref,
                 tri_ref, ones_ref,
                 o_ref, st_ref,
                 carry_ref, r_s, k_s, v_s, kk_s, b_s, ld_s, lc_s, g_s, y_s,
                 *, ns, tb, pre_shifted):
    @pl.when(pl.program_id(1) == 0)
    def _():
        carry_ref[...] = sh_ref[...]
        st_ref[...] = s0_ref[...]

    row = lax.broadcasted_iota(jnp.int32, (tb, 1), 0)

    def token_shift(x_ref, c0):
        if pre_shifted:
            return jnp.concatenate([x_ref[s] for s in range(ns)], axis=0)
        width = x_ref.shape[-1]
        outs = []
        for s in range(ns):
            x = x_ref[s]
            prev = jnp.where(row == 0, carry_ref[s, :, c0:c0 + width], pltpu.roll(x, 1, 0))
            outs.append(x + (prev - x) * mu_ref[:, c0:c0 + width])
            carry_ref[s, :, c0:c0 + width] = x[tb - 1:tb, :]
        return jnp.concatenate(outs, axis=0)

    ones = ones_ref[...]
    r = token_shift(pr_ref, 0)
    k = token_shift(pk_ref, D_RWKV)
    v = token_shift(pv_ref, 2 * D_RWKV)
    lo = token_shift(plo_ref, 3 * D_RWKV)

    lane = lax.broadcasted_iota(jnp.int32, (ns * tb, 128), 1)
    lo_wa = lo[:, 0:128]
    act_wa = jnp.where(lane < LORA_W, jnp.tanh(lo_wa), lo_wa).astype(BF16)
    act_g = jax.nn.sigmoid(lo[:, 128:384]).astype(BF16)
    w = w0_ref[...] + _dot(act_wa, wup_ref[...])
    ld = (-math.exp(-0.5)) * jax.nn.sigmoid(w)
    a = jax.nn.sigmoid(a0_ref[...] + _dot(act_wa, aup_ref[...]))
    g_s[...] = _dot(act_g, gup_ref[...])

    kk = k * kkw_ref[...]
    kk = kk * jnp.minimum(lax.rsqrt(_segsum(kk * kk, ones)), 1e12)
    k = k * (1.0 + (a - 1.0) * kaw_ref[...])

    tri = tri_ref[...]
    h1 = ld.astype(BF16)
    r1 = ld - h1.astype(F32)
    h2 = r1.astype(BF16)
    h3 = (r1 - h2.astype(F32)).astype(BF16)
    for s in range(ns):
        rows = slice(s * tb, (s + 1) * tb)
        lc_s[rows, :] = _dot(tri, h1[rows]) + (_dot(tri, h2[rows]) + _dot(tri, h3[rows]))
    ld_s[...] = ld
    r_s[...] = r
    k_s[...] = k
    v_s[...] = v
    kk_s[...] = kk
    b_s[...] = kk * a

    rr = lax.broadcasted_iota(jnp.int32, (2 * CHUNK, 2 * CHUNK), 0)
    cc = lax.broadcasted_iota(jnp.int32, (2 * CHUNK, 2 * CHUNK), 1)
    causal = cc % CHUNK < rr % CHUNK + jnp.where(rr < CHUNK, 1, 0)
    t_i = lax.broadcasted_iota(jnp.int32, (CHUNK, CHUNK), 0)
    s_i = lax.broadcasted_iota(jnp.int32, (CHUNK, CHUNK), 1)

    def same_block(size):
        return t_i // size == s_i // size

    diag_mask = jnp.where(same_block(_SOLVE_BASE), 1.0, 0.0).astype(F32)
    merge_masks = []
    size = _SOLVE_BASE
    while size < CHUNK:
        merge_masks.append(jnp.where(
            same_block(2 * size) & jnp.logical_not(same_block(size)), 1.0, 0.0).astype(F32))
        size *= 2
    lane_z = lax.broadcasted_iota(jnp.int32, (CHUNK, 4 * HEAD), 1)
    eye = jnp.where(t_i == s_i, 1.0, 0.0).astype(F32)

    def chunk_body(c, _):
        c0 = pl.multiple_of(c * CHUNK, CHUNK)
        units = []
        per_seq = []
        for s in range(ns):
            sl = pl.ds(s * tb + c0, CHUNK)
            lc = lc_s[sl, :]
            e_inc = jnp.exp(lc)
            e_exc = jnp.exp(lc - ld_s[sl, :])
            e_inv = jnp.exp(-lc)
            lc_end = lc[CHUNK - 1:CHUNK, :]
            e_hat = jnp.exp(lc_end - lc)
            kc = k_s[sl, :]
            bc = b_s[sl, :]
            per_seq.append(dict(
                sl=sl, e_end=jnp.exp(lc_end), vc=v_s[sl, :], rt=r_s[sl, :] * e_inc,
                kap=kk_s[sl, :] * e_exc, kt=kc * e_inv, bt=bc * e_inv, khat=kc * e_hat,
                bhat=bc * e_hat))
            units += [(s, h, slice(HEAD * h, HEAD * (h + 1))) for h in range(N_HEADS)]

        gms = []
        for s, h, hs in units:
            q = per_seq[s]
            lhs = jnp.concatenate([q["rt"][:, hs], q["kap"][:, hs]], axis=0)
            rhs = jnp.concatenate([q["kt"][:, hs], q["bt"][:, hs]], axis=0)
            gms.append(jnp.where(causal, _mm(lhs, rhs, ((1,), (1,))), 0.0))
        nmats = [-gm[CHUNK:, HEAD:2 * HEAD] for gm in gms]
        zeds = []
        for (s, h, hs), gm, nm in zip(units, gms, nmats):
            q = per_seq[s]
            akk_v = _mm(gm[CHUNK:, 0:HEAD], q["vc"][:, hs])
            zeds.append(jnp.concatenate(
                [nm * diag_mask, eye, akk_v, q["kap"][:, hs]], axis=1))
        steps = _SOLVE_BASE.bit_length() - 1
        for it in range(steps):
            ress = [_mm(zed[:, 0:HEAD], zed) for zed in zeds]
            if it < steps - 1:
                zeds = [jnp.where(lane_z < HEAD, res, zed + res) for zed, res in zip(zeds, ress)]
            else:
                zeds = [zed + res for zed, res in zip(zeds, ress)]
        for mask in merge_masks:
            ress = [_mm(nm * mask, zed) for nm, zed in zip(nmats, zeds)]
            ress = [_mm(zed[:, HEAD:2 * HEAD], res) for zed, res in zip(zeds, ress)]
            zeds = [zed + res for zed, res in zip(zeds, ress)]
        s_hs = [st_ref[s, h] for s, h, _ in units]
        ds = [_mm(jnp.concatenate([per_seq[s]["rt"][:, hs], zed[:, 3 * HEAD:4 * HEAD]], axis=0),
                  s_h, ((1,), (1,)))
              for (s, h, hs), zed, s_h in zip(units, zeds, s_hs)]
        vus = [jnp.concatenate([per_seq[s]["vc"][:, hs],
                                -(zed[:, 2 * HEAD:3 * HEAD] + d[CHUNK:, :])], axis=0)
               for (s, h, hs), zed, d in zip(units, zeds, ds)]
        for (s, h, hs), gm, d, vu in zip(units, gms, ds, vus):
            y_s[per_seq[s]["sl"], hs] = d[:CHUNK, :] + _mm(gm[:CHUNK, :], vu)
        for (s, h, hs), s_h, vu in zip(units, s_hs, vus):
            q = per_seq[s]
            kb_hat = jnp.concatenate([q["khat"][:, hs], q["bhat"][:, hs]], axis=0)
            st_ref[s, h] = s_h * q["e_end"][:, hs] + _mm(vu, kb_hat, ((0,), (0,)))
        return 0

    lax.fori_loop(0, tb // CHUNK, chunk_body, 0)

    y = y_s[...]
    mu = _segsum(y, ones) * (1.0 / HEAD)
    yc = y - mu
    var = _segsum(yc * yc, ones) * (1.0 / HEAD)
    yn = yc * lax.rsqrt(var + GN_EPS) * gng_ref[...] + gnb_ref[...]
    bonus = _segsum(r_s[...] * k_s[...] * rkw_ref[...], ones) * v_s[...]
    o_ref[...] = ((yn + bonus) * g_s[...]).astype(o_ref.dtype).reshape(ns, tb, D_RWKV)


_SOLVE_BASE = 8
_RWKV_SEQS = 2


def _rwkv(proj3, shift0p, wkv0, p, pre_shifted):
    bsz, seq, _ = proj3.shape
    ns = _RWKV_SEQS
    assert bsz % ns == 0
    tb = _RWKV_TOKENS if seq % _RWKV_TOKENS == 0 else CHUNK
    nl = seq // tb

    def col(cb, width=D_RWKV):
        return pl.BlockSpec((ns, tb, width), lambda b, l, cb=cb: (b, l, cb))

    consts = [p["mu_p"], p["w0"], p["wup"], p["a0"], p["aup"], p["gup"], p["k_k"], p["k_a"],
              p["r_k"], p["gn_g"], p["gn_b"], p["tri"][:tb, :tb], p["ones"]]
    state_spec = pl.BlockSpec((ns, N_HEADS, HEAD, HEAD), lambda b, l: (b, 0, 0, 0))
    scratch = ([pltpu.VMEM((ns, 1, N_SHIFT_PAD), F32)]
               + [pltpu.VMEM((ns * tb, D_RWKV), F32)] * 9)
    return pl.pallas_call(
        functools.partial(_rwkv_kernel, ns=ns, tb=tb, pre_shifted=pre_shifted),
        grid=(bsz // ns, nl),
        in_specs=[col(0), col(1), col(2), col(COL_LORA, N_LORA_PAD),
                  pl.BlockSpec((ns, 1, N_SHIFT_PAD), lambda b, l: (b, 0, 0)),
                  state_spec] + [_const_spec(c) for c in consts],
        out_specs=[pl.BlockSpec((ns, tb, D_RWKV), lambda b, l: (b, l, 0)), state_spec],
        out_shape=[jax.ShapeDtypeStruct((bsz, seq, D_RWKV), BF16),
                   jax.ShapeDtypeStruct((bsz, N_HEADS, HEAD, HEAD), F32)],
        scratch_shapes=scratch,
        compiler_params=pltpu.CompilerParams(
            dimension_semantics=("parallel", "arbitrary"), vmem_limit_bytes=VMEM_LIMIT),
    )(proj3, proj3, proj3, proj3, shift0p, wkv0, *consts)


def _s5_kernel(u_ref, hre0_ref, him0_ref, lre_ref, lim_ref, wbre_ref, wbim_ref, wcre_ref,
               wcim_ref, d_ref, gluw_ref, glub_ref,
               o_ref, hre_ref, him_ref,
               *scratch, ns, tb):
    xre, xim = scratch[:ns], scratch[ns:]
    n_col = S5_WIDE // 128
    per_bundle = S5_BUNDLE * S5_STATE // 128

    @pl.when(pl.program_id(1) == 0)
    def _():
        hre_ref[...] = hre0_ref[...]
        him_ref[...] = him0_ref[...]

    kin = S5_BUNDLE * S5_GROUP
    lre = lre_ref[...]
    lim = lim_ref[...]

    def col(j):
        return pl.ds(j, tb, stride=_S5_TOK_ROWS)

    def in_map(s):
        u = u_ref[s]
        for m in range(S5_GROUPS // S5_BUNDLE):
            um = u[:, kin * m:kin * (m + 1)].astype(BF16)
            res_re = _dot(um, wbre_ref[m])
            res_im = _dot(um, wbim_ref[m])
            for c in range(per_bundle):
                xre[s][col(per_bundle * m + c), :] = res_re[:, 128 * c:128 * (c + 1)]
                xim[s][col(per_bundle * m + c), :] = res_im[:, 128 * c:128 * (c + 1)]

    def scan():
        state = [(hre_ref[s], him_ref[s]) for s in range(ns)]
        for t in range(tb):
            rows = slice(t * _S5_TOK_ROWS, t * _S5_TOK_ROWS + n_col)
            for s in range(ns):
                hre, him = state[s]
                hre, him = (lre * hre - lim * him + xre[s][rows, :],
                            lre * him + lim * hre + xim[s][rows, :])
                xre[s][rows, :] = hre
                xim[s][rows, :] = him
                state[s] = (hre, him)
        for s in range(ns):
            hre_ref[s], him_ref[s] = state[s]

    def out_map(s):
        ys = []
        for m in range(S5_GROUPS // S5_BUNDLE):
            cols = range(per_bundle * m, per_bundle * (m + 1))
            h_re = jnp.concatenate([xre[s][col(j), :] for j in cols], axis=1).astype(BF16)
            h_im = jnp.concatenate([xim[s][col(j), :] for j in cols], axis=1).astype(BF16)
            ys.append(_dot(h_re, wcre_ref[m]) + _dot(h_im, wcim_ref[m]))
        y = jnp.concatenate(ys, axis=1) + d_ref[...] * u_ref[s]
        zg = 0.5 * y * (1.0 + jnp.tanh(math.sqrt(2.0 / math.pi) * (y + 0.044715 * (y * y * y))))
        gate = jax.nn.sigmoid(_dot(zg.astype(BF16), gluw_ref[...]) + glub_ref[...])
        o_ref[s] = (zg * gate).astype(o_ref.dtype)

    for s in range(ns):
        in_map(s)
    scan()
    for s in range(ns):
        out_map(s)


_S5_SEQS = 4
_S5_TOK_ROWS = 40


def _s5(proj3, hre0, him0, p):
    bsz, seq, _ = proj3.shape
    ns = _S5_SEQS
    assert bsz % ns == 0
    tb = _S5_TOKENS if seq % _S5_TOKENS == 0 else CHUNK
    nl = seq // tb
    consts = [p["lam_re"], p["lam_im"], p["wb_re"], p["wb_im"], p["wc_re"], p["wc_im"],
              p["s5_d"], p["glu_w"], p["glu_b"]]
    state_spec = pl.BlockSpec((ns, S5_WIDE // 128, 128), lambda b, l: (b, 0, 0))
    return pl.pallas_call(
        functools.partial(_s5_kernel, ns=ns, tb=tb),
        grid=(bsz // ns, nl),
        in_specs=[pl.BlockSpec((ns, tb, D_S5), lambda b, l: (b, l, COL_U)),
                  state_spec, state_spec] + [_const_spec(c) for c in consts],
        out_specs=[pl.BlockSpec((ns, tb, D_S5), lambda b, l: (b, l, 0)), state_spec, state_spec],
        out_shape=[jax.ShapeDtypeStruct((bsz, seq, D_S5), BF16),
                   jax.ShapeDtypeStruct((bsz, S5_WIDE // 128, 128), F32),
                   jax.ShapeDtypeStruct((bsz, S5_WIDE // 128, 128), F32)],
        scratch_shapes=[pltpu.VMEM((tb * _S5_TOK_ROWS, 128), F32)] * (2 * ns),
        compiler_params=pltpu.CompilerParams(
            dimension_semantics=("parallel", "arbitrary"), vmem_limit_bytes=VMEM_LIMIT),
    )(proj3, hre0, him0, *consts)


def _out_ln_kernel(x_ref, orw_ref, os5_ref, ging_ref, binb_ref, wa_ref, wb_ref, g_ref, b_ref, o_ref):
    for r in range(0, x_ref.shape[0], _LN_ROWS):
        rows = slice(r, r + _LN_ROWS)
        xn = _layer_norm(x_ref[rows, :], ging_ref[...], binb_ref[...])
        mix = _dot(orw_ref[rows, :], wa_ref[...]) + _dot(os5_ref[rows, :], wb_ref[...])
        o_ref[rows, :] = _layer_norm(ALPHA * xn + mix, g_ref[...], b_ref[...])


def _out_ln(x2d, o_rw, o_s5, p):
    t, d = x2d.shape
    tm = _row_tile(t, _DENSE_ROWS)
    consts = [p["ln_in_g"], p["ln_in_b"], p["w_out_a"], p["w_out_b"], p["ln1_g"], p["ln1_b"]]
    return pl.pallas_call(
        _out_ln_kernel,
        grid=(t // tm,),
        in_specs=[pl.BlockSpec((tm, d), lambda i: (i, 0)),
                  pl.BlockSpec((tm, D_RWKV), lambda i: (i, 0)),
                  pl.BlockSpec((tm, D_S5), lambda i: (i, 0))] + [_const_spec(c) for c in consts],
        out_specs=pl.BlockSpec((tm, d), lambda i: (i, 0)),
        out_shape=jax.ShapeDtypeStruct((t, d), F32),
        compiler_params=pltpu.CompilerParams(
            dimension_semantics=("parallel",), vmem_limit_bytes=VMEM_LIMIT),
    )(x2d, o_rw, o_s5, *consts)


def _ffn_ln_kernel(x_ref, w1_ref, w3_ref, w2_ref, g_ref, b_ref, o_ref, xb_ref, *, nf):
    f = pl.program_id(1)

    @pl.when(f == 0)
    def _():
        xb_ref[...] = x_ref[...].astype(BF16)
        o_ref[...] = jnp.zeros_like(o_ref)

    xb = xb_ref[...]
    h1 = _dot(xb, w1_ref[...])
    h3 = _dot(xb, w3_ref[...])
    act = (h1 * jax.nn.sigmoid(h1) * h3).astype(BF16)
    o_ref[...] += _dot(act, w2_ref[...])

    @pl.when(f == nf - 1)
    def _():
        o_ref[...] = _layer_norm(ALPHA * x_ref[...] + o_ref[...], g_ref[...], b_ref[...])


_FFN_TF = 512


def _ffn_ln(x1, p):
    t, d = x1.shape
    dff = p["ffn_w1"].shape[1]
    tm = _row_tile(t, _DENSE_ROWS)
    tf = _FFN_TF
    nf = dff // tf
    return pl.pallas_call(
        functools.partial(_ffn_ln_kernel, nf=nf),
        grid=(t // tm, nf),
        in_specs=[pl.BlockSpec((tm, d), lambda i, f: (i, 0)),
                  pl.BlockSpec((d, tf), lambda i, f: (0, f)),
                  pl.BlockSpec((d, tf), lambda i, f: (0, f)),
                  pl.BlockSpec((tf, d), lambda i, f: (f, 0)),
                  pl.BlockSpec((1, d), lambda i, f: (0, 0)),
                  pl.BlockSpec((1, d), lambda i, f: (0, 0))],
        out_specs=pl.BlockSpec((tm, d), lambda i, f: (i, 0)),
        out_shape=jax.ShapeDtypeStruct((t, d), F32),
        scratch_shapes=[pltpu.VMEM((tm, d), BF16)],
        compiler_params=pltpu.CompilerParams(
            dimension_semantics=("parallel", "arbitrary"), vmem_limit_bytes=VMEM_LIMIT),
    )(x1, p["ffn_w1"], p["ffn_w3"], p["ffn_w2"], p["ln2_g"], p["ln2_b"])


def _pad_cols(x, n):
    return jnp.pad(x, [(0, 0)] * (x.ndim - 1) + [(0, n - x.shape[-1])])


def _shift_layout(x):
    return jnp.concatenate([x[..., :3 * D_RWKV], _pad_cols(x[..., 3 * D_RWKV:], N_LORA_PAD)], axis=-1)


def _proj_layout(x):
    return jnp.concatenate(
        [x[..., :3 * D_RWKV], jnp.zeros(x.shape[:-1] + (D_S5,), x.dtype),
         _pad_cols(x[..., 3 * D_RWKV:], N_LORA_PAD)], axis=-1)


def _block_diag(blocks):
    nb, g, a, b = blocks.shape
    eye = jnp.eye(g, dtype=blocks.dtype)
    return jnp.einsum("mgab,gh->mgahb", blocks, eye).reshape(nb, g * a, g * b)


def _prepare(ln_in_g, ln_in_b, w_in, mu_shift, w0, w_lora_up, a0, a_lora_up, g_lora_up, k_k, k_a,
             r_k, gn_g, gn_b, s5_a_re, s5_a_im, s5_log_dt, s5_b_re, s5_b_im, s5_c_re, s5_c_im,
             s5_d, glu_w, glu_b, w_out, ln1_g, ln1_b, ffn_w1, ffn_w3, ffn_w2, ln2_g, ln2_b):
    row = lambda x: x.reshape(1, -1).astype(F32)
    w = w_in[0]
    o3 = 3 * D_RWKV
    w_in_p = jnp.concatenate(
        [w[:, :o3], w[:, N_SHIFT:], _pad_cols(w[:, o3:N_SHIFT], N_LORA_PAD)], axis=1).astype(BF16)
    wup = jnp.zeros((128, D_RWKV), F32).at[:LORA_W].set(w_lora_up[0]).astype(BF16)
    aup = jnp.zeros((128, D_RWKV), F32).at[LORA_W:LORA_W + LORA_A].set(a_lora_up[0]).astype(BF16)
    gup = jnp.zeros((256, D_RWKV), F32).at[:LORA_G].set(g_lora_up[0]).astype(BF16)
    idx = jnp.arange(256)
    tri = ((idx[:, None] >= idx[None, :]) & (idx[:, None] // CHUNK == idx[None, :] // CHUNK)).astype(BF16)
    ones = (idx[:, None] // HEAD == idx[None, :] // HEAD).astype(BF16)

    dt = jnp.exp(s5_log_dt[0].astype(F32))[:, None]
    a_re, a_im = s5_a_re[0].astype(F32), s5_a_im[0].astype(F32)
    mag = jnp.exp(a_re * dt)
    lam_re, lam_im = mag * jnp.cos(a_im * dt), mag * jnp.sin(a_im * dt)
    den = a_re * a_re + a_im * a_im
    q_re = ((lam_re - 1.0) * a_re + lam_im * a_im) / den
    q_im = (lam_im * a_re - (lam_re - 1.0) * a_im) / den
    b_re, b_im = s5_b_re[0].astype(F32), s5_b_im[0].astype(F32)
    bb_re = q_re[..., None] * b_re - q_im[..., None] * b_im
    bb_im = q_re[..., None] * b_im + q_im[..., None] * b_re
    nb = S5_GROUPS // S5_BUNDLE
    to_in = lambda x: _block_diag(jnp.swapaxes(x, 1, 2).reshape(nb, S5_BUNDLE, S5_GROUP, S5_STATE))
    to_out = lambda x: _block_diag(jnp.swapaxes(x, 1, 2).reshape(nb, S5_BUNDLE, S5_STATE, S5_GROUP))
    return {
        "ln_in_g": row(ln_in_g), "ln_in_b": row(ln_in_b), "w_in_p": w_in_p,
        "mu_p": _shift_layout(row(mu_shift[0])), "mu_proj": _proj_layout(row(mu_shift[0])),
        "w0": row(w0[0]), "wup": wup, "a0": row(a0[0]),
        "aup": aup, "gup": gup, "k_k": row(k_k[0]), "k_a": row(k_a[0]), "r_k": row(r_k[0]),
        "gn_g": row(gn_g[0]), "gn_b": row(gn_b[0]), "tri": tri, "ones": ones,
        "lam_re": lam_re.reshape(-1, 128), "lam_im": lam_im.reshape(-1, 128),
        "wb_re": to_in(bb_re).astype(BF16), "wb_im": to_in(bb_im).astype(BF16),
        "wc_re": to_out(s5_c_re[0].astype(F32)).astype(BF16),
        "wc_im": to_out(-s5_c_im[0].astype(F32)).astype(BF16),
        "s5_d": row(s5_d[0]), "glu_w": glu_w[0].astype(BF16), "glu_b": row(glu_b[0]),
        "w_out_a": w_out[0, :D_RWKV].astype(BF16), "w_out_b": w_out[0, D_RWKV:].astype(BF16),
        "ln1_g": row(ln1_g[0]), "ln1_b": row(ln1_b[0]),
        "ffn_w1": ffn_w1[0].astype(BF16), "ffn_w3": ffn_w3[0].astype(BF16),
        "ffn_w2": ffn_w2[0].astype(BF16), "ln2_g": row(ln2_g[0]), "ln2_b": row(ln2_b[0]),
    }


def _run(x, shift, wkv, h_re, h_im, p):
    bsz, seq, d = x.shape
    x2d = x.reshape(bsz * seq, d)
    in_proj = seq % _PROJ_ROWS == 0
    proj, last = _ln_proj(
        x2d, p["ln_in_g"], p["ln_in_b"], p["w_in_p"],
        (p["mu_proj"], _proj_layout(shift[0]), seq) if in_proj else None)
    proj3 = proj.reshape(bsz, seq, N_PROJ_PAD)
    o_rw, n_wkv = _rwkv(proj3, _shift_layout(shift[0]), wkv[0], p, in_proj)
    o_s5, n_hre, n_him = _s5(proj3, h_re[0].reshape(bsz, S5_WIDE // 128, 128),
                             h_im[0].reshape(bsz, S5_WIDE // 128, 128), p)
    x1 = _out_ln(x2d, o_rw.reshape(bsz * seq, D_RWKV), o_s5.reshape(bsz * seq, D_S5), p)
    y = _ffn_ln(x1, p).reshape(bsz, seq, d)
    if last is None:
        last = proj3[:, seq - 1:, :]
    lo0 = COL_LORA * N_LORA_PAD
    n_shift = jnp.concatenate([last[..., :3 * D_RWKV], last[..., lo0:lo0 + N_LORA]], axis=-1)
    return (y, n_shift[None], n_wkv[None],
            n_hre.reshape(1, bsz, S5_GROUPS, S5_STATE), n_him.reshape(1, bsz, S5_GROUPS, S5_STATE))


def kernel(x_prompt, x_sample, cache_shift, state_wkv, state_s5_re, state_s5_im, ln_in_g, ln_in_b, w_in, mu_shift, w0, w_lora_up, a0, a_lora_up, g_lora_up, k_k, k_a, r_k, gn_g, gn_b, s5_a_re, s5_a_im, s5_log_dt, s5_b_re, s5_b_im, s5_c_re, s5_c_im, s5_d, glu_w, glu_b, w_out, ln1_g, ln1_b, ffn_w1, ffn_w3, ffn_w2, ln2_g, ln2_b):
    p = _prepare(ln_in_g, ln_in_b, w_in, mu_shift, w0, w_lora_up, a0, a_lora_up, g_lora_up, k_k,
                 k_a, r_k, gn_g, gn_b, s5_a_re, s5_a_im, s5_log_dt, s5_b_re, s5_b_im, s5_c_re,
                 s5_c_im, s5_d, glu_w, glu_b, w_out, ln1_g, ln1_b, ffn_w1, ffn_w3, ffn_w2, ln2_g,
                 ln2_b)
    bp = x_prompt.shape[0]
    dt = x_prompt.dtype
    zeros = lambda *s: jnp.zeros((1, bp) + s, dt)
    yp, sh_p, wkv_p, re_p, im_p = _run(
        x_prompt, zeros(1, N_SHIFT), zeros(N_HEADS, HEAD, HEAD), zeros(S5_GROUPS, S5_STATE),
        zeros(S5_GROUPS, S5_STATE), p)
    ys, sh_s, wkv_s, re_s, im_s = _run(x_sample, cache_shift, state_wkv, state_s5_re, state_s5_im, p)
    return (yp, ys, sh_p, wkv_p, re_p, im_p, sh_s, wkv_s, re_s, im_s)
```
